```python
import jax, jax.numpy as jnp
from jax import lax
import numpy as np

D_MODEL = 4096
BATCH = 1
SEQ = 16384
DEPTH = 4

N_A_LAYERS = DEPTH // 2
N_B_LAYERS = DEPTH - N_A_LAYERS
POOL_WINDOWS = (2, 4, 8, 16)
N_POOL_GROUPS = len(POOL_WINDOWS)
POOL_GROUP_DIM = D_MODEL // N_POOL_GROUPS
HEAD_DIM = 128
N_HEADS = D_MODEL // HEAD_DIM
Q_BLOCK = 128
N_EXPERTS = 32
TOP_K = 4
D_EXPERT = 256
SWIGLU_LIMIT = 7.0
SWIGLU_ALPHA = 1.702
N_MOD = 6
EPS = 1e-6

kernel_name = "pool_fox_yoco_moe_adaln_trunk"


def rms_norm(x, g):
    xf = x.astype(jnp.float32)
    y = xf * lax.rsqrt(jnp.mean(xf * xf, axis=-1, keepdims=True) + EPS)
    return y.astype(x.dtype) * g


def modulate(x, g, shift, scale):
    return rms_norm(x, g) * (1 + scale[:, None, :]) + shift[:, None, :]


def causal_multiscale_pool(u):
    s = u.shape[1]
    cs = jnp.cumsum(u.astype(jnp.float32), axis=1)
    cs = jnp.concatenate([jnp.zeros_like(cs[:, :1]), cs], axis=1)
    t = jnp.arange(1, s + 1)[:, None]
    w = jnp.array(POOL_WINDOWS)[None, :]
    lo = jnp.maximum(t - w, 0)
    g_idx = jnp.arange(N_POOL_GROUPS)[None, :]
    win_sum = cs[:, 1:] - cs[:, lo, g_idx, :]
    count = (t - lo).astype(jnp.float32)[None, :, :, None]
    return (win_sum / count - u.astype(jnp.float32)).astype(u.dtype)


def pool_mixer(h, w_in, w_grp, b_grp, scale, w_out):
    b, s, d = h.shape
    u = (h @ w_in).reshape(b, s, N_POOL_GROUPS, POOL_GROUP_DIM)
    p = causal_multiscale_pool(u)
    y = jnp.einsum('bsgc,gce->bsge', p, w_grp) + b_grp
    y = (y * scale).reshape(b, s, d)
    return y @ w_out


def shared_kv(x, kv_norm_g, w_kvf, b_f, k_norm_g):
    b, s, d = x.shape
    z = rms_norm(x, kv_norm_g)
    kvf = z @ w_kvf
    k = rms_norm(kvf[..., :d].reshape(b, s, N_HEADS, HEAD_DIM), k_norm_g)
    v = kvf[..., d:2 * d].reshape(b, s, N_HEADS, HEAD_DIM)
    log_f = jax.nn.log_sigmoid((kvf[..., 2 * d:] + b_f).astype(jnp.float32))
    cum_log_f = jnp.cumsum(log_f, axis=1)
    return (k.transpose(0, 2, 1, 3), v.transpose(0, 2, 1, 3),
            cum_log_f.transpose(0, 2, 1))


def forgetting_attention(q, k, v, cum_log_f):
    b, h, s, dh = q.shape
    n_blocks = s // Q_BLOCK
    key_pos = jnp.arange(s)
    sm_scale = HEAD_DIM ** -0.5

    def block(i):
        start = i * Q_BLOCK
        q_blk = lax.dynamic_slice_in_dim(q, start, Q_BLOCK, axis=2)
        f_blk = lax.dynamic_slice_in_dim(cum_log_f, start, Q_BLOCK, axis=2)
        logits = jnp.einsum('bhqd,bhkd->bhqk', q_blk, k,
                            preferred_element_type=jnp.float32) * sm_scale
        logits = logits + f_blk[..., :, None] - cum_log_f[..., None, :]
        q_pos = start + jnp.arange(Q_BLOCK)
        causal = key_pos[None, :] <= q_pos[:, None]
        p = jax.nn.softmax(jnp.where(causal, logits, -jnp.inf), axis=-1)
        return jnp.einsum('bhqk,bhkd->bhqd', p.astype(v.dtype), v)

    out = lax.map(block, jnp.arange(n_blocks))
    return out.transpose(1, 0, 3, 2, 4).reshape(b, s, h * dh)


def fox_mixer(h, k, v, cum_log_f, w_qg, q_norm_g, w_o):
    b, s, d = h.shape
    qg = h @ w_qg
    q = rms_norm(qg[..., :d].reshape(b, s, N_HEADS, HEAD_DIM), q_norm_g).transpose(0, 2, 1, 3)
    o = forgetting_attention(q, k, v, cum_log_f)
    return (o * jax.nn.sigmoid(qg[..., d:])) @ w_o


def moe_ffn(h, w_router, b_router, w_gu, b_gu, w_dn, b_dn):
    logits = (h @ w_router).astype(jnp.float32) + b_router
    top_vals, top_idx = lax.top_k(logits, TOP_K)
    top_w = jax.nn.softmax(top_vals, axis=-1)
    combine = jnp.sum(jax.nn.one_hot(top_idx, N_EXPERTS, dtype=jnp.float32)
                      * top_w[..., None], axis=-2).astype(h.dtype)
    out = jnp.zeros_like(h)
    for e in range(N_EXPERTS):
        gu = h @ w_gu[e] + b_gu[e]
        gate = jnp.minimum(gu[..., 0::2], SWIGLU_LIMIT)
        lin = jnp.clip(gu[..., 1::2], -SWIGLU_LIMIT, SWIGLU_LIMIT)
        act = (lin + 1) * gate * jax.nn.sigmoid(SWIGLU_ALPHA * gate)
        out = out + combine[..., e:e + 1] * (act @ w_dn[e] + b_dn[e])
    return out


def setup_inputs(seed: int = 0) -> dict:
    key = jax.random.key(seed)
    ks = jax.random.split(key, 25)
    d = D_MODEL
    f32 = jnp.float32

    def nrm(k, shape, fan_in, gain=1.0):
        return jax.random.normal(k, shape, f32) * (gain * fan_in ** -0.5)

    def rnd(k, shape, s):
        return jax.random.normal(k, shape, f32) * s

    return {
        "x": jax.random.normal(ks[0], (BATCH, SEQ, d), f32),
        "c": jax.random.normal(ks[1], (BATCH, d), f32),
        "w_ada": nrm(ks[2], (d, N_MOD * d), d, 0.5),
        "b_ada": rnd(ks[3], (N_MOD * d,), 0.01),
        "ada_table": rnd(ks[4], (DEPTH, N_MOD, d), 0.1),
        "norm1_g": 1.0 + rnd(ks[5], (DEPTH, d), 0.05),
        "norm2_g": 1.0 + rnd(ks[6], (DEPTH, d), 0.05),
        "a_w_in": nrm(ks[7], (N_A_LAYERS, d, d), d),
        "a_w_grp": nrm(ks[8], (N_A_LAYERS, N_POOL_GROUPS, POOL_GROUP_DIM, POOL_GROUP_DIM), POOL_GROUP_DIM),
        "a_b_grp": rnd(ks[9], (N_A_LAYERS, N_POOL_GROUPS, POOL_GROUP_DIM), 0.01),
        "a_scale": 1.0 + rnd(ks[10], (N_A_LAYERS, N_POOL_GROUPS, POOL_GROUP_DIM), 0.05),
        "a_w_out": nrm(ks[11], (N_A_LAYERS, d, d), d),
        "kv_norm_g": 1.0 + rnd(ks[12], (d,), 0.05),
        "w_kvf": nrm(ks[13], (d, 2 * d + N_HEADS), d),
        "b_f": jax.random.uniform(ks[14], (N_HEADS,), f32, 1.0, 5.0),
        "k_norm_g": 1.0 + rnd(ks[15], (HEAD_DIM,), 0.05),
        "b_w_qg": nrm(ks[16], (N_B_LAYERS, d, 2 * d), d),
        "q_norm_g": 1.0 + rnd(ks[17], (N_B_LAYERS, HEAD_DIM), 0.05),
        "b_w_o": nrm(ks[18], (N_B_LAYERS, d, d), d),
        "moe_w_router": nrm(ks[19], (DEPTH, d, N_EXPERTS), d),
        "moe_b_router": rnd(ks[20], (DEPTH, N_EXPERTS), 0.01),
        "moe_w_gu": nrm(ks[21], (DEPTH, N_EXPERTS, d, 2 * D_EXPERT), d),
        "moe_b_gu": rnd(ks[22], (DEPTH, N_EXPERTS, 2 * D_EXPERT), 0.01),
        "moe_w_dn": nrm(ks[23], (DEPTH, N_EXPERTS, D_EXPERT, d), D_EXPERT),
        "moe_b_dn": rnd(ks[24], (DEPTH, N_EXPERTS, d), 0.01),
    }


def reference(x, c, w_ada, b_ada, ada_table, norm1_g, norm2_g, a_w_in, a_w_grp, a_b_grp,
              a_scale, a_w_out, kv_norm_g, w_kvf, b_f, k_norm_g, b_w_qg, q_norm_g, b_w_o,
              moe_w_router, moe_b_router, moe_w_gu, moe_b_gu, moe_w_dn, moe_b_dn):
    b = x.shape[0]
    mod = (jax.nn.silu(c) @ w_ada + b_ada).reshape(b, N_MOD, D_MODEL)
    k_sh = v_sh = f_sh = None
    for layer in range(DEPTH):
        m = mod + ada_table[layer]
        shift1, scale1, gate1, shift2, scale2, gate2 = [m[:, i] for i in range(N_MOD)]
        h = modulate(x, norm1_g[layer], shift1, scale1)
        if layer < N_A_LAYERS:
            y = pool_mixer(h, a_w_in[layer], a_w_grp[layer], a_b_grp[layer],
                           a_scale[layer], a_w_out[layer])
        else:
            j = layer - N_A_LAYERS
            y = fox_mixer(h, k_sh, v_sh, f_sh, b_w_qg[j], q_norm_g[j], b_w_o[j])
        x = x + gate1[:, None, :] * y
        h = modulate(x, norm2_g[layer], shift2, scale2)
        x = x + gate2[:, None, :] * moe_ffn(h, moe_w_router[layer], moe_b_router[layer],
                                            moe_w_gu[layer], moe_b_gu[layer],
                                            moe_w_dn[layer], moe_b_dn[layer])
        if layer == N_A_LAYERS - 1:
            k_sh, v_sh, f_sh = shared_kv(x, kv_norm_g, w_kvf, b_f, k_norm_g)
    return x
```

```python
import functools

import jax
import jax.numpy as jnp
from jax import lax
from jax.experimental import pallas as pl
from jax.experimental.pallas import tpu as pltpu

F32 = jnp.float32
BF16 = jnp.bfloat16

HEAD_DIM = 128
POOL_WINDOWS = (2, 4, 8, 16)
TOP_K = 4
SWIGLU_LIMIT = 7.0
SWIGLU_ALPHA = 1.702
EPS = 1e-6
LOG2E = 1.4426950408889634

AUG_DIM = 2 * HEAD_DIM
F32_EXP_UNDERFLOW = 88.0
MASK_VALUE = -1e30
MIB = 1 << 20


def _tile(n, pref, align):
    if n <= pref:
        return n
    t = (pref // align) * align
    while t >= align:
        if n % t == 0:
            return t
        t -= align
    raise ValueError(f"no tile for {n} (pref {pref}, align {align})")


def _params(semantics, vmem_mib):
    return pltpu.CompilerParams(dimension_semantics=semantics, vmem_limit_bytes=vmem_mib * MIB)


def _split3(a):
    hi = a.astype(BF16)
    r1 = a - hi.astype(F32)
    mid = r1.astype(BF16)
    lo = (r1 - mid.astype(F32)).astype(BF16)
    return hi, mid, lo


def _dot(a, b):
    return jnp.dot(a, b, preferred_element_type=F32)


def _dot_f32ish(a, b):
    ah = a.astype(BF16)
    al = (a - ah.astype(F32)).astype(BF16)
    bh = b.astype(BF16)
    bl = (b - bh.astype(F32)).astype(BF16)
    return _dot(ah, bh) + _dot(al, bh) + _dot(ah, bl)


def _modulated_norm(x, gs, sh):
    ms = jnp.mean(x * x, axis=-1, keepdims=True)
    return x * lax.rsqrt(ms + EPS) * gs + sh


NORM_ROWS = 64


def _norm_rows_into(x_ref, g_ref, sc_ref, sh_ref, h_ref):
    tm = x_ref.shape[0]
    rows = min(NORM_ROWS, tm)
    gs = g_ref[...] * (1.0 + sc_ref[...])
    sh = sh_ref[...]

    def body(r, carry):
        sl = pl.ds(pl.multiple_of(r * rows, rows), rows)
        h_ref[sl, :] = _modulated_norm(x_ref[sl, :], gs, sh).astype(h_ref.dtype)
        return carry

    lax.fori_loop(0, tm // rows, body, 0)


def _ada_kernel(c_ref, w_ref, b_ref, o_ref):
    c = c_ref[...]
    s = c * jax.nn.sigmoid(c)
    o_ref[...] = jnp.sum(w_ref[...] * s, axis=0, keepdims=True) + b_ref[...]


def _ada(c, w_ada, b_ada):
    d, n = w_ada.shape
    tn = _tile(n, 512, 128)
    return pl.pallas_call(
        _ada_kernel,
        grid=(n // tn,),
        in_specs=[pl.BlockSpec((d, 1), lambda j: (0, 0)),
                  pl.BlockSpec((d, tn), lambda j: (0, j)),
                  pl.BlockSpec((1, tn), lambda j: (0, j))],
        out_specs=pl.BlockSpec((1, tn), lambda j: (0, j)),
        out_shape=jax.ShapeDtypeStruct((1, n), F32),
        compiler_params=_params(("arbitrary",), 40),
        name="ada_proj",
    )(c.reshape(d, 1), w_ada, b_ada.reshape(1, n))


def _nmm_kernel(x_ref, g_ref, sc_ref, sh_ref, w_ref, o_ref, h_ref):
    @pl.when(pl.program_id(1) == 0)
    def _():
        _norm_rows_into(x_ref, g_ref, sc_ref, sh_ref, h_ref)

    o_ref[...] = _dot(h_ref[...], w_ref[...]).astype(o_ref.dtype)


def _nmm_side_kernel(x_ref, g_ref, sc_ref, sh_ref, w_ref, ws_ref, bs_ref, o_ref, os_ref, h_ref):
    @pl.when(pl.program_id(1) == 0)
    def _():
        _norm_rows_into(x_ref, g_ref, sc_ref, sh_ref, h_ref)
        os_ref[...] = _dot(h_ref[...], ws_ref[...]) + bs_ref[...]

    o_ref[...] = _dot(h_ref[...], w_ref[...]).astype(o_ref.dtype)


def _norm_matmul(x, g, sc, sh, w, side=None):
    s, d = x.shape
    n = w.shape[1]
    tm = _tile(s, 512, 16)
    tn = _tile(n, 512, 128)
    row = lambda i, j: (0, 0)
    in_specs = [pl.BlockSpec((tm, d), lambda i, j: (i, 0)),
                pl.BlockSpec((1, d), row), pl.BlockSpec((1, d), row), pl.BlockSpec((1, d), row),
                pl.BlockSpec((d, tn), lambda i, j: (0, j))]
    out_spec = pl.BlockSpec((tm, tn), lambda i, j: (i, j))
    out_shape = jax.ShapeDtypeStruct((s, n), BF16)
    scratch = [pltpu.VMEM((tm, d), BF16)]
    cp = _params(("arbitrary", "arbitrary"), 48)
    if side is None:
        return pl.pallas_call(_nmm_kernel, grid=(s // tm, n // tn), in_specs=in_specs,
                              out_specs=out_spec, out_shape=out_shape, scratch_shapes=scratch,
                              compiler_params=cp, name="norm_matmul")(x, g, sc, sh, w)
    ws, bs = side
    ns = ws.shape[1]
    in_specs += [pl.BlockSpec((d, ns), row), pl.BlockSpec((1, ns), row)]
    return pl.pallas_call(
        _nmm_side_kernel, grid=(s // tm, n // tn), in_specs=in_specs,
        out_specs=[out_spec, pl.BlockSpec((tm, ns), lambda i, j: (i, 0))],
        out_shape=[out_shape, jax.ShapeDtypeStruct((s, ns), F32)],
        scratch_shapes=scratch, compiler_params=cp, name="norm_matmul_side",
    )(x, g, sc, sh, w, ws, bs)


POOL_HALO = 16


def _inpool_kernel(x_ref, g_ref, sc_ref, sh_ref, w_ref, o_ref, h_ref, u_ref, carry_ref, *,
                   cols_per_group):
    i = pl.program_id(0)
    j = pl.program_id(1)
    tm, tn = o_ref.shape

    @pl.when(j == 0)
    def _():
        _norm_rows_into(x_ref, g_ref, sc_ref, sh_ref, h_ref)

    u = _dot(h_ref[...], w_ref[...])

    @pl.when(i == 0)
    def _():
        u_ref[0:POOL_HALO, :] = jnp.zeros((POOL_HALO, tn), F32)

    @pl.when(i > 0)
    def _():
        u_ref[0:POOL_HALO, :] = carry_ref[j]

    u_ref[POOL_HALO:, :] = u
    carry_ref[j] = u[tm - POOL_HALO:, :]

    t1 = (i * tm + lax.broadcasted_iota(jnp.int32, (tm, 1), 0) + 1).astype(F32)
    group = (j * tn) // cols_per_group
    for gi, w in enumerate(POOL_WINDOWS):
        @pl.when(group == gi)
        def _(w=w):
            acc = u_ref[POOL_HALO:, :]
            for k in range(1, w):
                acc = acc + u_ref[POOL_HALO - k:POOL_HALO - k + tm, :]
            cnt = jnp.minimum(t1, float(w))
            o_ref[...] = (acc / cnt - u_ref[POOL_HALO:, :]).astype(o_ref.dtype)


def _in_proj_pool(x, g, sc, sh, w_in):
    s, d = x.shape
    n = w_in.shape[1]
    cg = n // len(POOL_WINDOWS)
    tm = _tile(s, 512, 16)
    tn = _tile(cg, 512, 128)
    row = lambda i, j: (0, 0)
    return pl.pallas_call(
        functools.partial(_inpool_kernel, cols_per_group=cg),
        grid=(s // tm, n // tn),
        in_specs=[pl.BlockSpec((tm, d), lambda i, j: (i, 0)),
                  pl.BlockSpec((1, d), row), pl.BlockSpec((1, d), row), pl.BlockSpec((1, d), row),
                  pl.BlockSpec((d, tn), lambda i, j: (0, j))],
        out_specs=pl.BlockSpec((tm, tn), lambda i, j: (i, j)),
        out_shape=jax.ShapeDtypeStruct((s, n), BF16),
        scratch_shapes=[pltpu.VMEM((tm, d), BF16),
                        pltpu.VMEM((POOL_HALO + tm, tn), F32),
                        pltpu.VMEM((n // tn, POOL_HALO, tn), F32)],
        compiler_params=_params(("arbitrary", "arbitrary"), 48),
        name="in_proj_pool",
    )(x, g, sc, sh, w_in)


def _poolout_kernel(p_ref, wg_ref, bg_ref, sg_ref, wo_ref, x_ref, gate_ref, o_ref, y_ref):
    @pl.when(pl.program_id(1) == 0)
    def _():
        n_groups, cg, _ = wg_ref.shape
        for gi in range(n_groups):
            cols = slice(gi * cg, (gi + 1) * cg)
            y = _dot(p_ref[:, cols], wg_ref[gi])
            y_ref[:, cols] = ((y + bg_ref[gi]) * sg_ref[gi]).astype(y_ref.dtype)

    o_ref[...] = x_ref[...] + gate_ref[...] * _dot(y_ref[...], wo_ref[...])


def _pool_out(p, w_grp, b_grp, scale, w_out, x, gate):
    s, d = x.shape
    ng, cg, _ = w_grp.shape
    tm = _tile(s, 512, 16)
    tn = _tile(d, 512, 128)
    const3 = lambda i, j: (0, 0, 0)
    return pl.pallas_call(
        _poolout_kernel,
        grid=(s // tm, d // tn),
        in_specs=[pl.BlockSpec((tm, d), lambda i, j: (i, 0)),
                  pl.BlockSpec((ng, cg, cg), const3),
                  pl.BlockSpec((ng, 1, cg), const3),
                  pl.BlockSpec((ng, 1, cg), const3),
                  pl.BlockSpec((d, tn), lambda i, j: (0, j)),
                  pl.BlockSpec((tm, tn), lambda i, j: (i, j)),
                  pl.BlockSpec((1, tn), lambda i, j: (0, j))],
        out_specs=pl.BlockSpec((tm, tn), lambda i, j: (i, j)),
        out_shape=jax.ShapeDtypeStruct((s, d), F32),
        scratch_shapes=[pltpu.VMEM((tm, d), BF16)],
        compiler_params=_params(("arbitrary", "arbitrary"), 52),
        name="pool_out",
    )(p, w_grp, b_grp.reshape(ng, 1, cg), scale.reshape(ng, 1, cg), w_out, x, gate)


def _router_combine(logits):
    n_e = logits.shape[-1]
    lane = lax.broadcasted_iota(jnp.int32, logits.shape, 1)
    work = logits
    vals, idxs = [], []
    for _ in range(TOP_K):
        m = jnp.max(work, axis=-1, keepdims=True)
        idx = jnp.min(jnp.where(work == m, lane, n_e), axis=-1, keepdims=True)
        vals.append(m)
        idxs.append(idx)
        work = jnp.where(lane == idx, -jnp.inf, work)
    ex = [jnp.exp(v - vals[0]) for v in vals]
    den = ex[0]
    for e in ex[1:]:
        den = den + e
    comb = jnp.zeros(logits.shape, F32)
    for e, idx in zip(ex, idxs):
        comb = comb + jnp.where(lane == idx, e / den, 0.0)
    return comb


def _moe_dense_kernel(x_ref, g_ref, sc_ref, sh_ref, gate_ref, wr_ref, br_ref,
                      wg_ref, bg_ref, wl_ref, bl_ref, wd_ref, bd_ref,
                      o_ref, h_ref, lg_ref, comb_ref, acc_ref):
    e = pl.program_id(1)
    tm = x_ref.shape[0]

    @pl.when(e == 0)
    def _():
        rows = min(NORM_ROWS, tm)
        gs = g_ref[...] * (1.0 + sc_ref[...])
        sh = sh_ref[...]
        wr = wr_ref[...]
        wr_hi = wr.astype(BF16)
        wr_lo = (wr - wr_hi.astype(F32)).astype(BF16)

        def body(r, carry):
            sl = pl.ds(pl.multiple_of(r * rows, rows), rows)
            h = _modulated_norm(x_ref[sl, :], gs, sh)
            h_hi = h.astype(BF16)
            h_lo = (h - h_hi.astype(F32)).astype(BF16)
            h_ref[sl, :] = h_hi
            lg_ref[sl, :] = _dot(h_hi, wr_hi) + _dot(h_lo, wr_hi) + _dot(h_hi, wr_lo)
            return carry

        lax.fori_loop(0, tm // rows, body, 0)
        comb = _router_combine(lg_ref[...] + br_ref[...])
        comb_ref[...] = comb
        acc_ref[...] = _dot_f32ish(comb, bd_ref[...])

    lane = lax.broadcasted_iota(jnp.int32, comb_ref.shape, 1)
    c_e = jnp.sum(jnp.where(lane == e, comb_ref[...], 0.0), axis=-1, keepdims=True)
    h = h_ref[...]
    gt = jnp.minimum(_dot(h, wg_ref[...]) + bg_ref[...], SWIGLU_LIMIT)
    ln = jnp.clip(_dot(h, wl_ref[...]) + bl_ref[...], -SWIGLU_LIMIT, SWIGLU_LIMIT)
    act = (ln + 1.0) * gt * jax.nn.sigmoid(SWIGLU_ALPHA * gt)
    acc_ref[...] += _dot((act * c_e).astype(BF16), wd_ref[...])

    @pl.when(e == pl.num_programs(1) - 1)
    def _():
        o_ref[...] = x_ref[...] + gate_ref[...] * acc_ref[...]


def _moe_dense(x, g, sc, sh, gate, w_router, b_router, w_gate, b_gate, w_lin, b_lin, w_dn, b_dn):
    s, d = x.shape
    n_e, _, f = w_gate.shape
    tm = _tile(s, 256, 16)
    row = lambda i, e: (0, 0)
    per_e = lambda i, e: (e, 0, 0)
    return pl.pallas_call(
        _moe_dense_kernel,
        grid=(s // tm, n_e),
        in_specs=[pl.BlockSpec((tm, d), lambda i, e: (i, 0)),
                  pl.BlockSpec((1, d), row), pl.BlockSpec((1, d), row), pl.BlockSpec((1, d), row),
                  pl.BlockSpec((1, d), row),
                  pl.BlockSpec((d, n_e), row), pl.BlockSpec((1, n_e), row),
                  pl.BlockSpec((None, d, f), per_e), pl.BlockSpec((None, 1, f), per_e),
                  pl.BlockSpec((None, d, f), per_e), pl.BlockSpec((None, 1, f), per_e),
                  pl.BlockSpec((None, f, d), per_e),
                  pl.BlockSpec((n_e, d), row)],
        out_specs=pl.BlockSpec((tm, d), lambda i, e: (i, 0)),
        out_shape=jax.ShapeDtypeStruct((s, d), F32),
        scratch_shapes=[pltpu.VMEM((tm, d), BF16), pltpu.VMEM((tm, n_e), F32),
                        pltpu.VMEM((tm, n_e), F32), pltpu.VMEM((tm, d), F32)],
        compiler_params=_params(("arbitrary", "arbitrary"), 52),
        name="moe_dense",
    )(x, g, sc, sh, gate, w_router, b_router.reshape(1, n_e),
      w_gate, b_gate.reshape(n_e, 1, f), w_lin, b_lin.reshape(n_e, 1, f), w_dn, b_dn)


def _fcum_kernel(fl_ref, o_ref, carry_ref):
    @pl.when(pl.program_id(0) == 0)
    def _():
        carry_ref[...] = jnp.zeros(carry_ref.shape, F32)

    x = fl_ref[...]
    log_f = jnp.minimum(x, 0.0) - jnp.log1p(jnp.exp(-jnp.abs(x)))
    tm = x.shape[0]
    r = lax.broadcasted_iota(jnp.int32, (tm, tm), 0)
    c = lax.broadcasted_iota(jnp.int32, (tm, tm), 1)
    tri = jnp.where(c <= r, 1.0, 0.0).astype(BF16)
    hi, mid, lo = _split3(log_f)
    out = (_dot(tri, hi) + _dot(tri, mid) + _dot(tri, lo)) + carry_ref[...]
    o_ref[...] = out
    carry_ref[...] = out[tm - 1:tm, :]


def _forget_cumsum(fl):
    s, h = fl.shape
    tm = _tile(s, 256, 8)
    return pl.pallas_call(
        _fcum_kernel, grid=(s // tm,),
        in_specs=[pl.BlockSpec((tm, h), lambda i: (i, 0))],
        out_specs=pl.BlockSpec((tm, h), lambda i: (i, 0)),
        out_shape=jax.ShapeDtypeStruct((s, h), F32),
        scratch_shapes=[pltpu.VMEM((1, h), F32)],
        compiler_params=_params(("arbitrary",), 16),
        name="forget_cumsum",
    )(fl)


def _aug_kernel(src_ref, f_ref, g_ref, o_ref, *, mult, is_query):
    tm = src_ref.shape[0]
    n_heads = f_ref.shape[1]
    gm = g_ref[...] * mult
    lane = lax.broadcasted_iota(jnp.int32, (tm, HEAD_DIM), 1)
    for h in range(n_heads):
        t = src_ref[:, h * HEAD_DIM:(h + 1) * HEAD_DIM].astype(F32)
        ms = jnp.mean(t * t, axis=-1, keepdims=True)
        o_ref[:, h * AUG_DIM:h * AUG_DIM + HEAD_DIM] = (t * lax.rsqrt(ms + EPS) * gm).astype(BF16)
        f = jnp.broadcast_to(f_ref[:, h:h + 1] * LOG2E, (tm, HEAD_DIM))
        hi, mid, lo = [p.astype(F32) for p in _split3(f)]
        one = jnp.ones((tm, HEAD_DIM), F32)
        if is_query:
            pieces = (hi, mid, lo, one, one, one)
        else:
            pieces = (one, one, one, -hi, -mid, -lo)
        extra = jnp.zeros((tm, HEAD_DIM), F32)
        for p, piece in enumerate(pieces):
            extra = jnp.where(lane == p, piece, extra)
        o_ref[:, h * AUG_DIM + HEAD_DIM:(h + 1) * AUG_DIM] = extra.astype(BF16)


def _augment(src, f_cum, g, *, mult, is_query):
    s = src.shape[0]
    n_heads = f_cum.shape[1]
    d = n_heads * HEAD_DIM
    tm = _tile(s, 256, 16)
    return pl.pallas_call(
        functools.partial(_aug_kernel, mult=mult, is_query=is_query),
        grid=(s // tm,),
        in_specs=[pl.BlockSpec((tm, d), lambda i: (i, 0)),
                  pl.BlockSpec((tm, n_heads), lambda i: (i, 0)),
                  pl.BlockSpec((1, HEAD_DIM), lambda i: (0, 0))],
        out_specs=pl.BlockSpec((tm, n_heads * AUG_DIM), lambda i: (i, 0)),
        out_shape=jax.ShapeDtypeStruct((s, n_heads * AUG_DIM), BF16),
        compiler_params=_params(("arbitrary",), 32),
        name="augment_q" if is_query else "augment_k",
    )(src, f_cum, g.reshape(1, HEAD_DIM))


def _attn_kernel(start_ref, q_ref, k_ref, v_ref, o_ref, m_ref, l_ref, acc_ref):
    h = pl.program_id(0)
    qi = pl.program_id(1)
    tq = q_ref.shape[0]
    q = q_ref[...]
    m_ref[...] = jnp.full(m_ref.shape, MASK_VALUE, F32)
    l_ref[...] = jnp.zeros(l_ref.shape, F32)
    acc_ref[...] = jnp.zeros(acc_ref.shape, F32)

    def step(kb, on_diagonal):
        rows = pl.ds(pl.multiple_of(kb * tq, tq), tq)
        s = lax.dot_general(q, k_ref[rows, :], (((1,), (1,)), ((), ())),
                            preferred_element_type=F32)
        if on_diagonal:
            r = lax.broadcasted_iota(jnp.int32, s.shape, 0)
            c = lax.broadcasted_iota(jnp.int32, s.shape, 1)
            s = jnp.where(c <= r, s, MASK_VALUE)
        m_prev = m_ref[...]
        m_new = jnp.maximum(m_prev, jnp.max(s, axis=-1, keepdims=True))
        alpha = jnp.exp2(m_prev - m_new)
        p = jnp.exp2(s - m_new)
        l_ref[...] = alpha * l_ref[...] + jnp.sum(p, axis=-1, keepdims=True)
        acc_ref[...] = alpha * acc_ref[...] + _dot(p.astype(BF16), v_ref[rows, :])
        m_ref[...] = m_new

    def body(kb, carry):
        step(kb, False)
        return carry

    lax.fori_loop(start_ref[h, qi], qi, body, 0)
    step(qi, True)
    o_ref[...] = (acc_ref[...] / l_ref[...]).astype(o_ref.dtype)


def _attention(q_aug, k_aug, kv, start, n_heads, tq):
    s = q_aug.shape[0]
    nq = s // tq
    grid_spec = pltpu.PrefetchScalarGridSpec(
        num_scalar_prefetch=1,
        grid=(n_heads, nq),
        in_specs=[pl.BlockSpec((tq, AUG_DIM), lambda h, qi, st: (qi, h)),
                  pl.BlockSpec((s, AUG_DIM), lambda h, qi, st: (0, h)),
                  pl.BlockSpec((s, HEAD_DIM), lambda h, qi, st: (0, n_heads + h))],
        out_specs=pl.BlockSpec((tq, HEAD_DIM), lambda h, qi, st: (qi, h)),
        scratch_shapes=[pltpu.VMEM((tq, 1), F32), pltpu.VMEM((tq, 1), F32),
                        pltpu.VMEM((tq, HEAD_DIM), F32)],
    )
    return pl.pallas_call(
        _attn_kernel, grid_spec=grid_spec,
        out_shape=jax.ShapeDtypeStruct((s, n_heads * HEAD_DIM), BF16),
        compiler_params=_params(("arbitrary", "arbitrary"), 48),
        name="fox_attention",
    )(start, q_aug, k_aug, kv)


def _attention_start_blocks(f_cum, q_norm_g, k_norm_g, tq):
    s, n_heads = f_cum.shape
    nq = s // tq
    bound = HEAD_DIM ** 0.5 * jnp.max(jnp.abs(q_norm_g * k_norm_g))
    thresh = F32_EXP_UNDERFLOW + 2.0 * bound + 1.0
    fb = f_cum.reshape(nq, tq, n_heads)
    f_max_q = jnp.max(fb, axis=1).T
    f_min_k = jnp.min(fb, axis=1).T
    kb = jnp.arange(nq)
    needed = (f_max_q[:, :, None] - f_min_k[:, None, :]) >= -thresh
    needed = needed | (kb[None, None, :] >= kb[None, :, None])
    return jnp.argmax(needed, axis=-1).astype(jnp.int32)


def _gateout_kernel(o_ref, gp_ref, wo_ref, x_ref, gate_ref, out_ref, a_ref):
    @pl.when(pl.program_id(1) == 0)
    def _():
        a = o_ref[...].astype(F32) * jax.nn.sigmoid(gp_ref[...].astype(F32))
        a_ref[...] = a.astype(a_ref.dtype)

    out_ref[...] = x_ref[...] + gate_ref[...] * _dot(a_ref[...], wo_ref[...])


def _gate_out(o, qg, w_o, x, gate):
    s, d = x.shape
    tm = _tile(s, 512, 16)
    tn = _tile(d, 512, 128)
    return pl.pallas_call(
        _gateout_kernel,
        grid=(s // tm, d // tn),
        in_specs=[pl.BlockSpec((tm, d), lambda i, j: (i, 0)),
                  pl.BlockSpec((tm, d), lambda i, j: (i, 1)),
                  pl.BlockSpec((d, tn), lambda i, j: (0, j)),
                  pl.BlockSpec((tm, tn), lambda i, j: (i, j)),
                  pl.BlockSpec((1, tn), lambda i, j: (0, j))],
        out_specs=pl.BlockSpec((tm, tn), lambda i, j: (i, j)),
        out_shape=jax.ShapeDtypeStruct((s, d), F32),
        scratch_shapes=[pltpu.VMEM((tm, d), BF16)],
        compiler_params=_params(("arbitrary", "arbitrary"), 48),
        name="gate_out",
    )(o, qg, w_o, x, gate)


ATTN_BLOCK = 512


def kernel(x, c, w_ada, b_ada, ada_table, norm1_g, norm2_g, a_w_in, a_w_grp, a_b_grp, a_scale,
           a_w_out, kv_norm_g, w_kvf, b_f, k_norm_g, b_w_qg, q_norm_g, b_w_o, moe_w_router,
           moe_b_router, moe_w_gu, moe_b_gu, moe_w_dn, moe_b_dn):
    b, s, d = x.shape
    assert b == 1, "single-sequence trunk"
    depth = ada_table.shape[0]
    n_mod = ada_table.shape[1]
    n_a = a_w_in.shape[0]
    n_heads = d // HEAD_DIM
    tq = _tile(s, ATTN_BLOCK, 16)
    xs = x.reshape(s, d)

    mod = _ada(c, w_ada, b_ada).reshape(n_mod, d)
    zeros_row = jnp.zeros((1, d), F32)
    k_aug = kv = f_cum = None

    for layer in range(depth):
        m = mod + ada_table[layer]
        shift1, scale1, gate1, shift2, scale2, gate2 = [m[i:i + 1] for i in range(n_mod)]
        g1 = norm1_g[layer].reshape(1, d)
        g2 = norm2_g[layer].reshape(1, d)
        if layer < n_a:
            p = _in_proj_pool(xs, g1, scale1, shift1, a_w_in[layer].astype(BF16))
            xs = _pool_out(p, a_w_grp[layer].astype(BF16), a_b_grp[layer], a_scale[layer],
                           a_w_out[layer].astype(BF16), xs, gate1)
        else:
            j = layer - n_a
            qg = _norm_matmul(xs, g1, scale1, shift1, b_w_qg[j].astype(BF16))
            q_aug = _augment(qg, f_cum, q_norm_g[j], mult=HEAD_DIM ** -0.5 * LOG2E, is_query=True)
            start = _attention_start_blocks(f_cum, q_norm_g[j], k_norm_g, tq)
            o = _attention(q_aug, k_aug, kv, start, n_heads, tq)
            xs = _gate_out(o, qg, b_w_o[j].astype(BF16), xs, gate1)

        w_gu = moe_w_gu[layer]
        b_gu = moe_b_gu[layer]
        xs = _moe_dense(xs, g2, scale2, shift2, gate2, moe_w_router[layer], moe_b_router[layer],
                        w_gu[..., 0::2].astype(BF16), b_gu[..., 0::2],
                        w_gu[..., 1::2].astype(BF16), b_gu[..., 1::2],
                        moe_w_dn[layer].astype(BF16), moe_b_dn[layer])

        if layer == n_a - 1:
            kv, fl = _norm_matmul(xs, kv_norm_g.reshape(1, d), zeros_row, zeros_row,
                                  w_kvf[:, :2 * d].astype(BF16),
                                  side=(w_kvf[:, 2 * d:].astype(BF16), b_f.reshape(1, n_heads)))
            f_cum = _forget_cumsum(fl)
            k_aug = _augment(kv, f_cum, k_norm_g, mult=1.0, is_query=False)

    return xs.reshape(b, s, d)
```

```python
import functools

import jax
import jax.numpy as jnp
from jax import lax
from jax.experimental import pallas as pl
from jax.experimental.pallas import tpu as pltpu

F32 = jnp.float32
BF16 = jnp.bfloat16

HEAD_DIM = 128
POOL_WINDOWS = (2, 4, 8, 16)
TOP_K = 4
SWIGLU_LIMIT = 7.0
SWIGLU_ALPHA = 1.702
EPS = 1e-6
LOG2E = 1.4426950408889634

AUG_DIM = 2 * HEAD_DIM
F32_EXP_UNDERFLOW = 88.0
MASK_VALUE = -1e30
MIB = 1 << 20


def _tile(n, pref, align):
    if n <= pref:
        return n
    t = (pref // align) * align
    while t >= align:
        if n % t == 0:
            return t
        t -= align
    raise ValueError(f"no tile for {n} (pref {pref}, align {align})")


def _params(semantics, vmem_mib):
    return pltpu.CompilerParams(dimension_semantics=semantics, vmem_limit_bytes=vmem_mib * MIB)


def _split3(a):
    hi = a.astype(BF16)
    r1 = a - hi.astype(F32)
    mid = r1.astype(BF16)
    lo = (r1 - mid.astype(F32)).astype(BF16)
    return hi, mid, lo


def _dot(a, b):
    return jnp.dot(a, b, preferred_element_type=F32)


def _dot_f32ish(a, b):
    ah = a.astype(BF16)
    al = (a - ah.astype(F32)).astype(BF16)
    bh = b.astype(BF16)
    bl = (b - bh.astype(F32)).astype(BF16)
    return _dot(ah, bh) + _dot(al, bh) + _dot(ah, bl)


def _modulated_norm(x, gs, sh):
    ms = jnp.mean(x * x, axis=-1, keepdims=True)
    return x * lax.rsqrt(ms + EPS) * gs + sh


NORM_ROWS = 64


def _norm_rows_into(x_ref, g_ref, sc_ref, sh_ref, h_ref):
    tm = x_ref.shape[0]
    rows = min(NORM_ROWS, tm)
    gs = g_ref[...] * (1.0 + sc_ref[...])
    sh = sh_ref[...]

    def body(r, carry):
        sl = pl.ds(pl.multiple_of(r * rows, rows), rows)
        h_ref[sl, :] = _modulated_norm(x_ref[sl, :], gs, sh).astype(h_ref.dtype)
        return carry

    lax.fori_loop(0, tm // rows, body, 0)


def _ada_kernel(c_ref, w_ref, b_ref, o_ref):
    c = c_ref[...]
    s = c * jax.nn.sigmoid(c)
    o_ref[...] = jnp.sum(w_ref[...] * s, axis=0, keepdims=True) + b_ref[...]


def _ada(c, w_ada, b_ada):
    d, n = w_ada.shape
    tn = _tile(n, 512, 128)
    return pl.pallas_call(
        _ada_kernel,
        grid=(n // tn,),
        in_specs=[pl.BlockSpec((d, 1), lambda j: (0, 0)),
                  pl.BlockSpec((d, tn), lambda j: (0, j)),
                  pl.BlockSpec((1, tn), lambda j: (0, j))],
        out_specs=pl.BlockSpec((1, tn), lambda j: (0, j)),
        out_shape=jax.ShapeDtypeStruct((1, n), F32),
        compiler_params=_params(("arbitrary",), 40),
        name="ada_proj",
    )(c.reshape(d, 1), w_ada, b_ada.reshape(1, n))


def _nmm_kernel(x_ref, g_ref, sc_ref, sh_ref, w_ref, o_ref, h_ref):
    @pl.when(pl.program_id(1) == 0)
    def _():
        _norm_rows_into(x_ref, g_ref, sc_ref, sh_ref, h_ref)

    o_ref[...] = _dot(h_ref[...], w_ref[...]).astype(o_ref.dtype)


def _nmm_side_kernel(x_ref, g_ref, sc_ref, sh_ref, w_ref, ws_ref, bs_ref, o_ref, os_ref, h_ref):
    @pl.when(pl.program_id(1) == 0)
    def _():
        _norm_rows_into(x_ref, g_ref, sc_ref, sh_ref, h_ref)
        os_ref[...] = _dot(h_ref[...], ws_ref[...]) + bs_ref[...]

    o_ref[...] = _dot(h_ref[...], w_ref[...]).astype(o_ref.dtype)


def _norm_matmul(x, g, sc, sh, w, side=None):
    s, d = x.shape
    n = w.shape[1]
    tm = _tile(s, 512, 16)
    tn = _tile(n, 512, 128)
    row = lambda i, j: (0, 0)
    in_specs = [pl.BlockSpec((tm, d), lambda i, j: (i, 0)),
                pl.BlockSpec((1, d), row), pl.BlockSpec((1, d), row), pl.BlockSpec((1, d), row),
                pl.BlockSpec((d, tn), lambda i, j: (0, j))]
    out_spec = pl.BlockSpec((tm, tn), lambda i, j: (i, j))
    out_shape = jax.ShapeDtypeStruct((s, n), BF16)
    scratch = [pltpu.VMEM((tm, d), BF16)]
    cp = _params(("arbitrary", "arbitrary"), 48)
    if side is None:
        return pl.pallas_call(_nmm_kernel, grid=(s // tm, n // tn), in_specs=in_specs,
                              out_specs=out_spec, out_shape=out_shape, scratch_shapes=scratch,
                              compiler_params=cp, name="norm_matmul")(x, g, sc, sh, w)
    ws, bs = side
    ns = ws.shape[1]
    in_specs += [pl.BlockSpec((d, ns), row), pl.BlockSpec((1, ns), row)]
    return pl.pallas_call(
        _nmm_side_kernel, grid=(s // tm, n // tn), in_specs=in_specs,
        out_specs=[out_spec, pl.BlockSpec((tm, ns), lambda i, j: (i, 0))],
        out_shape=[out_shape, jax.ShapeDtypeStruct((s, ns), F32)],
        scratch_shapes=scratch, compiler_params=cp, name="norm_matmul_side",
    )(x, g, sc, sh, w, ws, bs)


POOL_HALO = 16


def _inpool_kernel(x_ref, g_ref, sc_ref, sh_ref, w_ref, o_ref, h_ref, u_ref, carry_ref, *,
                   cols_per_group):
    i = pl.program_id(0)
    j = pl.program_id(1)
    tm, tn = o_ref.shape

    @pl.when(j == 0)
    def _():
        _norm_rows_into(x_ref, g_ref, sc_ref, sh_ref, h_ref)

    u = _dot(h_ref[...], w_ref[...])

    @pl.when(i == 0)
    def _():
        u_ref[0:POOL_HALO, :] = jnp.zeros((POOL_HALO, tn), F32)

    @pl.when(i > 0)
    def _():
        u_ref[0:POOL_HALO, :] = carry_ref[j]

    u_ref[POOL_HALO:, :] = u
    carry_ref[j] = u[tm - POOL_HALO:, :]

    t1 = (i * tm + lax.broadcasted_iota(jnp.int32, (tm, 1), 0) + 1).astype(F32)
    group = (j * tn) // cols_per_group
    for gi, w in enumerate(POOL_WINDOWS):
        @pl.when(group == gi)
        def _(w=w):
            acc = u_ref[POOL_HALO:, :]
            for k in range(1, w):
                acc = acc + u_ref[POOL_HALO - k:POOL_HALO - k + tm, :]
            cnt = jnp.minimum(t1, float(w))
            o_ref[...] = (acc / cnt - u_ref[POOL_HALO:, :]).astype(o_ref.dtype)


def _in_proj_pool(x, g, sc, sh, w_in):
    s, d = x.shape
    n = w_in.shape[1]
    cg = n // len(POOL_WINDOWS)
    tm = _tile(s, 512, 16)
    tn = _tile(cg, 512, 128)
    row = lambda i, j: (0, 0)
    return pl.pallas_call(
        functools.partial(_inpool_kernel, cols_per_group=cg),
        grid=(s // tm, n // tn),
        in_specs=[pl.BlockSpec((tm, d), lambda i, j: (i, 0)),
                  pl.BlockSpec((1, d), row), pl.BlockSpec((1, d), row), pl.BlockSpec((1, d), row),
                  pl.BlockSpec((d, tn), lambda i, j: (0, j))],
        out_specs=pl.BlockSpec((tm, tn), lambda i, j: (i, j)),
        out_shape=jax.ShapeDtypeStruct((s, n), BF16),
        scratch_shapes=[pltpu.VMEM((tm, d), BF16),
                        pltpu.VMEM((POOL_HALO + tm, tn), F32),
                        pltpu.VMEM((n // tn, POOL_HALO, tn), F32)],
        compiler_params=_params(("arbitrary", "arbitrary"), 48),
        name="in_proj_pool",
    )(x, g, sc, sh, w_in)


def _poolout_kernel(p_ref, wg_ref, bg_ref, sg_ref, wo_ref, x_ref, gate_ref, o_ref, y_ref):
    @pl.when(pl.program_id(1) == 0)
    def _():
        n_groups, cg, _ = wg_ref.shape
        for gi in range(n_groups):
            cols = slice(gi * cg, (gi + 1) * cg)
            y = _dot(p_ref[:, cols], wg_ref[gi])
            y_ref[:, cols] = ((y + bg_ref[gi]) * sg_ref[gi]).astype(y_ref.dtype)

    o_ref[...] = x_ref[...] + gate_ref[...] * _dot(y_ref[...], wo_ref[...])


def _pool_out(p, w_grp, b_grp, scale, w_out, x, gate):
    s, d = x.shape
    ng, cg, _ = w_grp.shape
    tm = _tile(s, 512, 16)
    tn = _tile(d, 512, 128)
    const3 = lambda i, j: (0, 0, 0)
    return pl.pallas_call(
        _poolout_kernel,
        grid=(s // tm, d // tn),
        in_specs=[pl.BlockSpec((tm, d), lambda i, j: (i, 0)),
                  pl.BlockSpec((ng, cg, cg), const3),
                  pl.BlockSpec((ng, 1, cg), const3),
                  pl.BlockSpec((ng, 1, cg), const3),
                  pl.BlockSpec((d, tn), lambda i, j: (0, j)),
                  pl.BlockSpec((tm, tn), lambda i, j: (i, j)),
                  pl.BlockSpec((1, tn), lambda i, j: (0, j))],
        out_specs=pl.BlockSpec((tm, tn), lambda i, j: (i, j)),
        out_shape=jax.ShapeDtypeStruct((s, d), F32),
        scratch_shapes=[pltpu.VMEM((tm, d), BF16)],
        compiler_params=_params(("arbitrary", "arbitrary"), 52),
        name="pool_out",
    )(p, w_grp, b_grp.reshape(ng, 1, cg), scale.reshape(ng, 1, cg), w_out, x, gate)


LANES = 128
HIGH_HALF = 0xFFFF0000


def _pack_rows(v):
    half = v.shape[1] // 2
    lo = lax.bitcast_convert_type(v[:, :half].astype(BF16).astype(F32), jnp.uint32)
    hi = lax.bitcast_convert_type(v[:, half:].astype(BF16).astype(F32), jnp.uint32)
    return (lo >> 16) | (hi & jnp.uint32(HIGH_HALF))


def _unpack_lo(u):
    return lax.bitcast_convert_type(u << 16, F32)


def _unpack_hi(u):
    return lax.bitcast_convert_type(u & jnp.uint32(HIGH_HALF), F32)


def _slab(c, rows, sub):
    return pl.ds(c, rows, stride=sub)


def _top_k(logits):
    n_e = logits.shape[-1]
    lane = lax.broadcasted_iota(jnp.int32, logits.shape, 1)
    work = logits
    vals, idxs = [], []
    for _ in range(TOP_K):
        m = jnp.max(work, axis=-1, keepdims=True)
        idx = jnp.min(jnp.where(work == m, lane, n_e), axis=-1, keepdims=True)
        vals.append(m)
        idxs.append(idx)
        work = jnp.where(lane == idx, -jnp.inf, work)
    return vals, idxs


def _route_kernel(x_ref, g_ref, sc_ref, sh_ref, wr_ref, br_ref,
                  hp_ref, idx_ref, w_ref, pos_ref, cnt_ref, lg_ref, carry_ref):
    tm, d = x_ref.shape
    sub = hp_ref.shape[0] // tm
    n_e = wr_ref.shape[1]
    rows = min(NORM_ROWS, tm)

    @pl.when(pl.program_id(0) == 0)
    def _():
        carry_ref[...] = jnp.zeros(carry_ref.shape, F32)

    gs = g_ref[...] * (1.0 + sc_ref[...])
    sh = sh_ref[...]
    wr = wr_ref[...]
    wr_hi = wr.astype(BF16)
    wr_lo = (wr - wr_hi.astype(F32)).astype(BF16)
    for r in range(tm // rows):
        sl = slice(r * rows, (r + 1) * rows)
        h = _modulated_norm(x_ref[sl, :], gs, sh)
        h_hi = h.astype(BF16)
        h_lo = (h - h_hi.astype(F32)).astype(BF16)
        lg_ref[sl, :] = _dot(h_hi, wr_hi) + _dot(h_lo, wr_hi) + _dot(h_hi, wr_lo)
        packed = _pack_rows(h)
        for c in range(sub):
            hp_ref[pl.ds(r * rows * sub + c, rows, stride=sub), :] = packed[:, c * LANES:(c + 1) * LANES]

    vals, idxs = _top_k(lg_ref[...] + br_ref[...])
    ex = [jnp.exp(v - vals[0]) for v in vals]
    den = ex[0]
    for e in ex[1:]:
        den = den + e

    lane = lax.broadcasted_iota(jnp.int32, (tm, n_e), 1)
    sel = jnp.zeros((tm, n_e), F32)
    for idx in idxs:
        sel = jnp.where(lane == idx, 1.0, sel)
    r_i = lax.broadcasted_iota(jnp.int32, (tm, tm), 0)
    c_i = lax.broadcasted_iota(jnp.int32, (tm, tm), 1)
    below = jnp.where(c_i < r_i, 1.0, 0.0).astype(BF16)
    ranks = _dot(below, sel.astype(BF16)) + carry_ref[...]

    slot = lax.broadcasted_iota(jnp.int32, (tm, TOP_K), 1)
    idx_out = jnp.zeros((tm, TOP_K), jnp.int32)
    w_out = jnp.zeros((tm, TOP_K), F32)
    pos_out = jnp.zeros((tm, TOP_K), F32)
    for k in range(TOP_K):
        pk = jnp.sum(jnp.where(lane == idxs[k], ranks, 0.0), axis=-1, keepdims=True)
        idx_out = jnp.where(slot == k, idxs[k], idx_out)
        w_out = jnp.where(slot == k, ex[k] / den, w_out)
        pos_out = jnp.where(slot == k, pk, pos_out)
    idx_ref[...] = idx_out
    w_ref[...] = w_out
    pos_ref[...] = pos_out.astype(jnp.int32)
    carry_ref[...] += jnp.sum(sel, axis=0, keepdims=True)
    cnt_ref[...] = carry_ref[...]


def _route(x, g, sc, sh, w_router, b_router):
    s, d = x.shape
    n_e = w_router.shape[1]
    sub = d // 2 // LANES
    tm = _tile(s, 512, 16)
    row = lambda i: (0, 0)
    tok = lambda i: (i, 0)
    return pl.pallas_call(
        _route_kernel,
        grid=(s // tm,),
        in_specs=[pl.BlockSpec((tm, d), tok),
                  pl.BlockSpec((1, d), row), pl.BlockSpec((1, d), row), pl.BlockSpec((1, d), row),
                  pl.BlockSpec((d, n_e), row), pl.BlockSpec((1, n_e), row)],
        out_specs=[pl.BlockSpec((tm * sub, LANES), tok),
                   pl.BlockSpec((tm, TOP_K), tok), pl.BlockSpec((tm, TOP_K), tok),
                   pl.BlockSpec((tm, TOP_K), tok), pl.BlockSpec((1, n_e), row)],
        out_shape=[jax.ShapeDtypeStruct((s * sub, LANES), jnp.uint32),
                   jax.ShapeDtypeStruct((s, TOP_K), jnp.int32),
                   jax.ShapeDtypeStruct((s, TOP_K), F32),
                   jax.ShapeDtypeStruct((s, TOP_K), jnp.int32),
                   jax.ShapeDtypeStruct((1, n_e), F32)],
        scratch_shapes=[pltpu.VMEM((tm, n_e), F32), pltpu.VMEM((1, n_e), F32)],
        compiler_params=_params(("arbitrary",), 40),
        name="moe_route",
    )(x, g, sc, sh, w_router, b_router.reshape(1, n_e))


MOVE_CHUNK = 2048


def _move_kernel(si_hbm, di_hbm, src_hbm, dst_hbm, si_smem, di_smem, idx_sem, row_sem, *, sub):
    chunk = si_smem.shape[0]
    base = pl.multiple_of(pl.program_id(0) * chunk, chunk)
    si_copy = pltpu.make_async_copy(si_hbm.at[pl.ds(base, chunk)], si_smem, idx_sem.at[0])
    di_copy = pltpu.make_async_copy(di_hbm.at[pl.ds(base, chunk)], di_smem, idx_sem.at[1])
    si_copy.start()
    di_copy.start()
    si_copy.wait()
    di_copy.wait()

    def row_copy(m):
        src = pl.multiple_of(si_smem[m] * sub, sub)
        dst = pl.multiple_of(di_smem[m] * sub, sub)
        return pltpu.make_async_copy(src_hbm.at[pl.ds(src, sub), :], dst_hbm.at[pl.ds(dst, sub), :],
                                     row_sem)

    def issue(m, carry):
        row_copy(m).start()
        return carry

    def drain(m, carry):
        row_copy(m).wait()
        return carry

    lax.fori_loop(0, chunk, issue, 0)
    lax.fori_loop(0, chunk, drain, 0)


def _move_rows(src, src_idx, dst_idx, sub):
    m = src_idx.shape[0]
    chunk = _tile(m, MOVE_CHUNK, 1024)
    any_spec = pl.BlockSpec(memory_space=pl.ANY)
    return pl.pallas_call(
        functools.partial(_move_kernel, sub=sub),
        grid=(m // chunk,),
        in_specs=[any_spec, any_spec, any_spec],
        out_specs=any_spec,
        out_shape=jax.ShapeDtypeStruct((m * sub, LANES), src.dtype),
        scratch_shapes=[pltpu.SMEM((chunk,), jnp.int32), pltpu.SMEM((chunk,), jnp.int32),
                        pltpu.SemaphoreType.DMA((2,)), pltpu.SemaphoreType.DMA],
        compiler_params=_params(("arbitrary",), 16),
        name="move_rows",
    )(src_idx, dst_idx, src)


def _experts_kernel(vt_ref, ve_ref, lo_ref, hi_ref, hs_ref, wgu_ref, bgu_ref, wdn_ref, o_ref):
    v = pl.program_id(0)
    lo = lo_ref[v]
    hi = hi_ref[v]
    f, d = wdn_ref.shape
    half = d // 2
    sub = half // LANES
    tr = hs_ref.shape[0] // sub

    @pl.when(hi > lo)
    def _():
        hu = jnp.concatenate([hs_ref[_slab(c, tr, sub), :] for c in range(sub)], axis=1)
        h_lo = _unpack_lo(hu).astype(BF16)
        h_hi = _unpack_hi(hu).astype(BF16)
        gu = _dot(h_lo, wgu_ref[:half, :]) + _dot(h_hi, wgu_ref[half:, :]) + bgu_ref[...]
        gt = jnp.minimum(gu[:, :f], SWIGLU_LIMIT)
        ln = jnp.clip(gu[:, f:], -SWIGLU_LIMIT, SWIGLU_LIMIT)
        act = (ln + 1.0) * gt * jax.nn.sigmoid(SWIGLU_ALPHA * gt)
        packed = _pack_rows(_dot(act.astype(BF16), wdn_ref[...]))
        row = lax.broadcasted_iota(jnp.int32, (tr, 1), 0)
        mine = (row >= lo) & (row < hi)

        @pl.when(lo == 0)
        def _():
            for c in range(sub):
                o_ref[_slab(c, tr, sub), :] = jnp.where(mine, packed[:, c * LANES:(c + 1) * LANES],
                                                        jnp.uint32(0))

        @pl.when(lo > 0)
        def _():
            for c in range(sub):
                rows = _slab(c, tr, sub)
                o_ref[rows, :] = jnp.where(mine, packed[:, c * LANES:(c + 1) * LANES], o_ref[rows, :])


EXPERT_TILE = 256


def _experts(hs, visits, w_gu, b_gu, w_dn, tr, sub):
    vt, ve, vlo, vhi = visits
    n_e, d, f2 = w_gu.shape
    f = f2 // 2
    grid_spec = pltpu.PrefetchScalarGridSpec(
        num_scalar_prefetch=4,
        grid=(vt.shape[0],),
        in_specs=[pl.BlockSpec((tr * sub, LANES), lambda v, vt, ve, lo, hi: (vt[v], 0)),
                  pl.BlockSpec((None, d, f2), lambda v, vt, ve, lo, hi: (ve[v], 0, 0)),
                  pl.BlockSpec((None, 1, f2), lambda v, vt, ve, lo, hi: (ve[v], 0, 0)),
                  pl.BlockSpec((None, f, d), lambda v, vt, ve, lo, hi: (ve[v], 0, 0))],
        out_specs=pl.BlockSpec((tr * sub, LANES), lambda v, vt, ve, lo, hi: (vt[v], 0)),
    )
    return pl.pallas_call(
        _experts_kernel, grid_spec=grid_spec,
        out_shape=jax.ShapeDtypeStruct(hs.shape, jnp.uint32),
        compiler_params=_params(("arbitrary",), 48),
        name="moe_experts",
    )(vt, ve, vlo, vhi, hs, w_gu, b_gu.reshape(n_e, 1, f2), w_dn)


def _combine_kernel(y0_ref, y1_ref, y2_ref, y3_ref, w_ref, idx_ref, bd_ref, x_ref, gate_ref,
                    o_ref, bias_ref):
    tm, d = x_ref.shape
    half = d // 2
    sub = half // LANES
    n_e = bd_ref.shape[0]
    w = w_ref[...]
    idx = idx_ref[...]
    lane = lax.broadcasted_iota(jnp.int32, (tm, n_e), 1)
    comb = jnp.zeros((tm, n_e), F32)
    for k in range(TOP_K):
        comb = comb + jnp.where(lane == idx[:, k:k + 1], w[:, k:k + 1], 0.0)
    bias_ref[...] = _dot_f32ish(comb, bd_ref[...])
    for c in range(sub):
        lo = jnp.zeros((tm, LANES), F32)
        hi = jnp.zeros((tm, LANES), F32)
        for k, y_ref in enumerate((y0_ref, y1_ref, y2_ref, y3_ref)):
            u = y_ref[_slab(c, tm, sub), :]
            lo = lo + w[:, k:k + 1] * _unpack_lo(u)
            hi = hi + w[:, k:k + 1] * _unpack_hi(u)
        cl = slice(c * LANES, (c + 1) * LANES)
        ch = slice(half + c * LANES, half + (c + 1) * LANES)
        o_ref[:, cl] = x_ref[:, cl] + gate_ref[:, cl] * (lo + bias_ref[:, cl])
        o_ref[:, ch] = x_ref[:, ch] + gate_ref[:, ch] * (hi + bias_ref[:, ch])


def _combine(yt, w4, idx4, b_dn, x, gate):
    s, d = x.shape
    n_e = b_dn.shape[0]
    sub = d // 2 // LANES
    tm = _tile(s, 256, 16)
    nt = s // tm
    tok = lambda i: (i, 0)
    row = lambda i: (0, 0)
    slot_specs = [pl.BlockSpec((tm * sub, LANES), functools.partial(lambda i, k: (k * nt + i, 0), k=k))
                  for k in range(TOP_K)]
    return pl.pallas_call(
        _combine_kernel,
        grid=(nt,),
        in_specs=slot_specs + [pl.BlockSpec((tm, TOP_K), tok), pl.BlockSpec((tm, TOP_K), tok),
                               pl.BlockSpec((n_e, d), row), pl.BlockSpec((tm, d), tok),
                               pl.BlockSpec((1, d), row)],
        out_specs=pl.BlockSpec((tm, d), tok),
        out_shape=jax.ShapeDtypeStruct((s, d), F32),
        scratch_shapes=[pltpu.VMEM((tm, d), F32)],
        compiler_params=_params(("arbitrary",), 48),
        name="moe_combine",
    )(yt, yt, yt, yt, w4, idx4, b_dn, x, gate)


def _expert_visits(off, m, tr, n_e):
    nt = m // tr
    pts = jnp.sort(jnp.concatenate([jnp.arange(nt, dtype=jnp.int32) * tr, off[:n_e]]))
    nxt = jnp.concatenate([pts[1:], jnp.full((1,), m, jnp.int32)])
    vt = jnp.minimum(pts // tr, nt - 1)
    ve = jnp.clip(jnp.searchsorted(off[1:], pts, side="right"), 0, n_e - 1).astype(jnp.int32)
    return vt, ve, pts - vt * tr, nxt - vt * tr


def _moe(x, g, sc, sh, gate, w_router, b_router, w_gu, b_gu, w_dn, b_dn):
    s, d = x.shape
    n_e = w_router.shape[1]
    sub = d // 2 // LANES
    m = TOP_K * s
    tr = _tile(m, EXPERT_TILE, 16)
    hp, idx4, w4, pos4, counts = _route(x, g, sc, sh, w_router, b_router)
    counts = counts.reshape(n_e).astype(jnp.int32)
    off = jnp.concatenate([jnp.zeros((1,), jnp.int32), jnp.cumsum(counts)])
    dest4 = off[idx4] + pos4
    token = jnp.repeat(jnp.arange(s, dtype=jnp.int32), TOP_K)
    hs = _move_rows(hp, token, dest4.reshape(m), sub)
    ys = _experts(hs, _expert_visits(off, m, tr, n_e), w_gu, b_gu, w_dn, tr, sub)
    yt = _move_rows(ys, dest4.T.reshape(m), jnp.arange(m, dtype=jnp.int32), sub)
    return _combine(yt, w4, idx4, b_dn, x, gate)


def _prep_gu_kernel(w_ref, o_ref):
    n = w_ref.shape[1]
    src = lax.broadcasted_iota(jnp.int32, (n, n), 0)
    dst = lax.broadcasted_iota(jnp.int32, (n, n), 1)
    want = jnp.where(dst < n // 2, 2 * dst, 2 * (dst - n // 2) + 1)
    perm = jnp.where(src == want, 1.0, 0.0).astype(BF16)
    o_ref[...] = _dot(w_ref[...].astype(BF16), perm).astype(BF16)


def _prep_gu(w_gu):
    shape = w_gu.shape
    n = shape[-1]
    rows = w_gu.size // n
    tm = _tile(rows, 2048, 16)
    out = pl.pallas_call(
        _prep_gu_kernel, grid=(rows // tm,),
        in_specs=[pl.BlockSpec((tm, n), lambda i: (i, 0))],
        out_specs=pl.BlockSpec((tm, n), lambda i: (i, 0)),
        out_shape=jax.ShapeDtypeStruct((rows, n), BF16),
        compiler_params=_params(("arbitrary",), 32),
        name="prep_expert_up",
    )(w_gu.reshape(rows, n))
    return out.reshape(shape)


def _fcum_kernel(fl_ref, o_ref, carry_ref):
    @pl.when(pl.program_id(0) == 0)
    def _():
        carry_ref[...] = jnp.zeros(carry_ref.shape, F32)

    x = fl_ref[...]
    log_f = jnp.minimum(x, 0.0) - jnp.log1p(jnp.exp(-jnp.abs(x)))
    tm = x.shape[0]
    r = lax.broadcasted_iota(jnp.int32, (tm, tm), 0)
    c = lax.broadcasted_iota(jnp.int32, (tm, tm), 1)
    tri = jnp.where(c <= r, 1.0, 0.0).astype(BF16)
    hi, mid, lo = _split3(log_f)
    out = (_dot(tri, hi) + _dot(tri, mid) + _dot(tri, lo)) + carry_ref[...]
    o_ref[...] = out
    carry_ref[...] = out[tm - 1:tm, :]


def _forget_cumsum(fl):
    s, h = fl.shape
    tm = _tile(s, 256, 8)
    return pl.pallas_call(
        _fcum_kernel, grid=(s // tm,),
        in_specs=[pl.BlockSpec((tm, h), lambda i: (i, 0))],
        out_specs=pl.BlockSpec((tm, h), lambda i: (i, 0)),
        out_shape=jax.ShapeDtypeStruct((s, h), F32),
        scratch_shapes=[pltpu.VMEM((1, h), F32)],
        compiler_params=_params(("arbitrary",), 16),
        name="forget_cumsum",
    )(fl)


def _aug_kernel(src_ref, f_ref, g_ref, o_ref, *, mult, is_query):
    tm = src_ref.shape[0]
    n_heads = f_ref.shape[1]
    gm = g_ref[...] * mult
    lane = lax.broadcasted_iota(jnp.int32, (tm, HEAD_DIM), 1)
    for h in range(n_heads):
        t = src_ref[:, h * HEAD_DIM:(h + 1) * HEAD_DIM].astype(F32)
        ms = jnp.mean(t * t, axis=-1, keepdims=True)
        o_ref[:, h * AUG_DIM:h * AUG_DIM + HEAD_DIM] = (t * lax.rsqrt(ms + EPS) * gm).astype(BF16)
        f = jnp.broadcast_to(f_ref[:, h:h + 1] * LOG2E, (tm, HEAD_DIM))
        hi, mid, lo = [p.astype(F32) for p in _split3(f)]
        one = jnp.ones((tm, HEAD_DIM), F32)
        if is_query:
            pieces = (hi, mid, lo, one, one, one)
        else:
            pieces = (one, one, one, -hi, -mid, -lo)
        extra = jnp.zeros((tm, HEAD_DIM), F32)
        for p, piece in enumerate(pieces):
            extra = jnp.where(lane == p, piece, extra)
        o_ref[:, h * AUG_DIM + HEAD_DIM:(h + 1) * AUG_DIM] = extra.astype(BF16)


def _augment(src, f_cum, g, *, mult, is_query):
    s = src.shape[0]
    n_heads = f_cum.shape[1]
    d = n_heads * HEAD_DIM
    tm = _tile(s, 256, 16)
    return pl.pallas_call(
        functools.partial(_aug_kernel, mult=mult, is_query=is_query),
        grid=(s // tm,),
        in_specs=[pl.BlockSpec((tm, d), lambda i: (i, 0)),
                  pl.BlockSpec((tm, n_heads), lambda i: (i, 0)),
                  pl.BlockSpec((1, HEAD_DIM), lambda i: (0, 0))],
        out_specs=pl.BlockSpec((tm, n_heads * AUG_DIM), lambda i: (i, 0)),
        out_shape=jax.ShapeDtypeStruct((s, n_heads * AUG_DIM), BF16),
        compiler_params=_params(("arbitrary",), 32),
        name="augment_q" if is_query else "augment_k",
    )(src, f_cum, g.reshape(1, HEAD_DIM))


def _attn_kernel(start_ref, q_ref, k_ref, v_ref, o_ref, m_ref, l_ref, acc_ref, *, tk):
    h = pl.program_id(0)
    qi = pl.program_id(1)
    n_sub = q_ref.shape[0] // tk
    m_ref[...] = jnp.full(m_ref.shape, MASK_VALUE, F32)
    l_ref[...] = jnp.zeros(l_ref.shape, F32)
    acc_ref[...] = jnp.zeros(acc_ref.shape, F32)

    def sub_step(a, kb, on_diagonal):
        qrows = slice(a * tk, (a + 1) * tk)
        krows = pl.ds(pl.multiple_of(kb * tk, tk), tk)
        s = lax.dot_general(q_ref[qrows, :], k_ref[krows, :], (((1,), (1,)), ((), ())),
                            preferred_element_type=F32)
        if on_diagonal:
            r = lax.broadcasted_iota(jnp.int32, s.shape, 0)
            c = lax.broadcasted_iota(jnp.int32, s.shape, 1)
            s = jnp.where(c <= r, s, MASK_VALUE)
        m_prev = m_ref[qrows, :]
        m_new = jnp.maximum(m_prev, jnp.max(s, axis=-1, keepdims=True))
        alpha = jnp.exp2(m_prev - m_new)
        p = jnp.exp2(s - m_new)
        l_ref[qrows, :] = alpha * l_ref[qrows, :] + jnp.sum(p, axis=-1, keepdims=True)
        acc_ref[qrows, :] = alpha * acc_ref[qrows, :] + _dot(p.astype(BF16), v_ref[krows, :])
        m_ref[qrows, :] = m_new

    def body(kb, carry):
        for a in range(n_sub):
            sub_step(a, kb, False)
        return carry

    first_diag = qi * n_sub
    lax.fori_loop(start_ref[h, qi], first_diag, body, 0)
    for a in range(n_sub):
        for b in range(a):
            sub_step(a, first_diag + b, False)
        sub_step(a, first_diag + a, True)
    o_ref[...] = (acc_ref[...] / l_ref[...]).astype(o_ref.dtype)


def _attention(q_aug, k_aug, kv, start, n_heads, tq, tk):
    s = q_aug.shape[0]
    nq = s // tq
    grid_spec = pltpu.PrefetchScalarGridSpec(
        num_scalar_prefetch=1,
        grid=(n_heads, nq),
        in_specs=[pl.BlockSpec((tq, AUG_DIM), lambda h, qi, st: (qi, h)),
                  pl.BlockSpec((s, AUG_DIM), lambda h, qi, st: (0, h)),
                  pl.BlockSpec((s, HEAD_DIM), lambda h, qi, st: (0, n_heads + h))],
        out_specs=pl.BlockSpec((tq, HEAD_DIM), lambda h, qi, st: (qi, h)),
        scratch_shapes=[pltpu.VMEM((tq, 1), F32), pltpu.VMEM((tq, 1), F32),
                        pltpu.VMEM((tq, HEAD_DIM), F32)],
    )
    return pl.pallas_call(
        functools.partial(_attn_kernel, tk=tk), grid_spec=grid_spec,
        out_shape=jax.ShapeDtypeStruct((s, n_heads * HEAD_DIM), BF16),
        compiler_params=_params(("arbitrary", "arbitrary"), 48),
        name="fox_attention",
    )(start, q_aug, k_aug, kv)


def _attention_start_blocks(f_cum, q_norm_g, k_norm_g, tq, tk):
    s, n_heads = f_cum.shape
    nq = s // tq
    nk = s // tk
    bound = HEAD_DIM ** 0.5 * jnp.max(jnp.abs(q_norm_g * k_norm_g))
    thresh = F32_EXP_UNDERFLOW + 2.0 * bound + 1.0
    f_max_q = jnp.max(f_cum.reshape(nq, tq, n_heads), axis=1).T
    f_min_k = jnp.min(f_cum.reshape(nk, tk, n_heads), axis=1).T
    needed = (f_max_q[:, :, None] - f_min_k[:, None, :]) >= -thresh
    first_diag = jnp.arange(nq) * (tq // tk)
    needed = needed | (jnp.arange(nk)[None, None, :] >= first_diag[None, :, None])
    return jnp.argmax(needed, axis=-1).astype(jnp.int32)


def _gateout_kernel(o_ref, gp_ref, wo_ref, x_ref, gate_ref, out_ref, a_ref):
    @pl.when(pl.program_id(1) == 0)
    def _():
        a = o_ref[...].astype(F32) * jax.nn.sigmoid(gp_ref[...].astype(F32))
        a_ref[...] = a.astype(a_ref.dtype)

    out_ref[...] = x_ref[...] + gate_ref[...] * _dot(a_ref[...], wo_ref[...])


def _gate_out(o, qg, w_o, x, gate):
    s, d = x.shape
    tm = _tile(s, 512, 16)
    tn = _tile(d, 512, 128)
    return pl.pallas_call(
        _gateout_kernel,
        grid=(s // tm, d // tn),
        in_specs=[pl.BlockSpec((tm, d), lambda i, j: (i, 0)),
                  pl.BlockSpec((tm, d), lambda i, j: (i, 1)),
                  pl.BlockSpec((d, tn), lambda i, j: (0, j)),
                  pl.BlockSpec((tm, tn), lambda i, j: (i, j)),
                  pl.BlockSpec((1, tn), lambda i, j: (0, j))],
        out_specs=pl.BlockSpec((tm, tn), lambda i, j: (i, j)),
        out_shape=jax.ShapeDtypeStruct((s, d), F32),
        scratch_shapes=[pltpu.VMEM((tm, d), BF16)],
        compiler_params=_params(("arbitrary", "arbitrary"), 48),
        name="gate_out",
    )(o, qg, w_o, x, gate)


ATTN_KV_BLOCK = 512
ATTN_Q_SUBTILES = 2


def kernel(x, c, w_ada, b_ada, ada_table, norm1_g, norm2_g, a_w_in, a_w_grp, a_b_grp, a_scale,
           a_w_out, kv_norm_g, w_kvf, b_f, k_norm_g, b_w_qg, q_norm_g, b_w_o, moe_w_router,
           moe_b_router, moe_w_gu, moe_b_gu, moe_w_dn, moe_b_dn):
    b, s, d = x.shape
    assert b == 1, "single-sequence trunk"
    depth = ada_table.shape[0]
    n_mod = ada_table.shape[1]
    n_a = a_w_in.shape[0]
    n_heads = d // HEAD_DIM
    tk = _tile(s, ATTN_KV_BLOCK, 16)
    tq = tk * ATTN_Q_SUBTILES if s % (tk * ATTN_Q_SUBTILES) == 0 else tk
    xs = x.reshape(s, d)

    mod = _ada(c, w_ada, b_ada).reshape(n_mod, d)
    zeros_row = jnp.zeros((1, d), F32)
    w_gu_all = _prep_gu(moe_w_gu)
    b_gu_all = jnp.concatenate([moe_b_gu[..., 0::2], moe_b_gu[..., 1::2]], axis=-1)
    k_aug = kv = f_cum = None

    for layer in range(depth):
        m = mod + ada_table[layer]
        shift1, scale1, gate1, shift2, scale2, gate2 = [m[i:i + 1] for i in range(n_mod)]
        g1 = norm1_g[layer].reshape(1, d)
        g2 = norm2_g[layer].reshape(1, d)
        if layer < n_a:
            p = _in_proj_pool(xs, g1, scale1, shift1, a_w_in[layer].astype(BF16))
            xs = _pool_out(p, a_w_grp[layer].astype(BF16), a_b_grp[layer], a_scale[layer],
                           a_w_out[layer].astype(BF16), xs, gate1)
        else:
            j = layer - n_a
            qg = _norm_matmul(xs, g1, scale1, shift1, b_w_qg[j].astype(BF16))
            q_aug = _augment(qg, f_cum, q_norm_g[j], mult=HEAD_DIM ** -0.5 * LOG2E, is_query=True)
            start = _attention_start_blocks(f_cum, q_norm_g[j], k_norm_g, tq, tk)
            o = _attention(q_aug, k_aug, kv, start, n_heads, tq, tk)
            xs = _gate_out(o, qg, b_w_o[j].astype(BF16), xs, gate1)

        xs = _moe(xs, g2, scale2, shift2, gate2, moe_w_router[layer], moe_b_router[layer],
                  w_gu_all[layer], b_gu_all[layer], moe_w_dn[layer].astype(BF16), moe_b_dn[layer])

        if layer == n_a - 1:
            kv, fl = _norm_matmul(xs, kv_norm_g.reshape(1, d), zeros_row, zeros_row,
                                  w_kvf[:, :2 * d].astype(BF16),
                                  side=(w_kvf[:, 2 * d:].astype(BF16), b_f.reshape(1, n_heads)))
            f_cum = _forget_cumsum(fl)
            k_aug = _augment(kv, f_cum, k_norm_g, mult=1.0, is_query=False)

    return xs.reshape(b, s, d)
```

```python
import functools

import jax
import jax.numpy as jnp
from jax import lax
from jax.experimental import pallas as pl
from jax.experimental.pallas import tpu as pltpu

F32 = jnp.float32
BF16 = jnp.bfloat16

HEAD_DIM = 128
POOL_WINDOWS = (2, 4, 8, 16)
TOP_K = 4
SWIGLU_LIMIT = 7.0
SWIGLU_ALPHA = 1.702
EPS = 1e-6
LOG2E = 1.4426950408889634

AUG_DIM = 2 * HEAD_DIM
F32_EXP_UNDERFLOW = 88.0
MASK_VALUE = -1e30
MIB = 1 << 20


def _tile(n, pref, align):
    if n <= pref:
        return n
    t = (pref // align) * align
    while t >= align:
        if n % t == 0:
            return t
        t -= align
    raise ValueError(f"no tile for {n} (pref {pref}, align {align})")


def _params(semantics, vmem_mib):
    return pltpu.CompilerParams(dimension_semantics=semantics, vmem_limit_bytes=vmem_mib * MIB)


def _split3(a):
    hi = a.astype(BF16)
    r1 = a - hi.astype(F32)
    mid = r1.astype(BF16)
    lo = (r1 - mid.astype(F32)).astype(BF16)
    return hi, mid, lo


def _dot(a, b):
    return jnp.dot(a, b, preferred_element_type=F32)


def _dot_f32ish(a, b):
    ah = a.astype(BF16)
    al = (a - ah.astype(F32)).astype(BF16)
    bh = b.astype(BF16)
    bl = (b - bh.astype(F32)).astype(BF16)
    return _dot(ah, bh) + _dot(al, bh) + _dot(ah, bl)


def _modulated_norm(x, gs, sh):
    ms = jnp.mean(x * x, axis=-1, keepdims=True)
    return x * lax.rsqrt(ms + EPS) * gs + sh


NORM_ROWS = 64


def _norm_rows_into(x_ref, g_ref, sc_ref, sh_ref, h_ref):
    tm = x_ref.shape[0]
    rows = min(NORM_ROWS, tm)
    gs = g_ref[...] * (1.0 + sc_ref[...])
    sh = sh_ref[...]

    def body(r, carry):
        sl = pl.ds(pl.multiple_of(r * rows, rows), rows)
        h_ref[sl, :] = _modulated_norm(x_ref[sl, :], gs, sh).astype(h_ref.dtype)
        return carry

    lax.fori_loop(0, tm // rows, body, 0)


def _ada_kernel(c_ref, w_ref, b_ref, o_ref):
    c = c_ref[...]
    s = c * jax.nn.sigmoid(c)
    o_ref[...] = jnp.sum(w_ref[...] * s, axis=0, keepdims=True) + b_ref[...]


def _ada(c, w_ada, b_ada):
    d, n = w_ada.shape
    tn = _tile(n, 512, 128)
    return pl.pallas_call(
        _ada_kernel,
        grid=(n // tn,),
        in_specs=[pl.BlockSpec((d, 1), lambda j: (0, 0)),
                  pl.BlockSpec((d, tn), lambda j: (0, j)),
                  pl.BlockSpec((1, tn), lambda j: (0, j))],
        out_specs=pl.BlockSpec((1, tn), lambda j: (0, j)),
        out_shape=jax.ShapeDtypeStruct((1, n), F32),
        compiler_params=_params(("arbitrary",), 40),
        name="ada_proj",
    )(c.reshape(d, 1), w_ada, b_ada.reshape(1, n))


def _nmm_kernel(x_ref, g_ref, sc_ref, sh_ref, w_ref, o_ref, h_ref):
    @pl.when(pl.program_id(1) == 0)
    def _():
        _norm_rows_into(x_ref, g_ref, sc_ref, sh_ref, h_ref)

    o_ref[...] = _dot(h_ref[...], w_ref[...]).astype(o_ref.dtype)


def _nmm_side_kernel(x_ref, g_ref, sc_ref, sh_ref, w_ref, ws_ref, bs_ref, o_ref, os_ref, h_ref):
    @pl.when(pl.program_id(1) == 0)
    def _():
        _norm_rows_into(x_ref, g_ref, sc_ref, sh_ref, h_ref)
        os_ref[...] = _dot(h_ref[...], ws_ref[...]) + bs_ref[...]

    o_ref[...] = _dot(h_ref[...], w_ref[...]).astype(o_ref.dtype)


def _norm_matmul(x, g, sc, sh, w, side=None):
    s, d = x.shape
    n = w.shape[1]
    tm = _tile(s, 512, 16)
    tn = _tile(n, 512, 128)
    row = lambda i, j: (0, 0)
    in_specs = [pl.BlockSpec((tm, d), lambda i, j: (i, 0)),
                pl.BlockSpec((1, d), row), pl.BlockSpec((1, d), row), pl.BlockSpec((1, d), row),
                pl.BlockSpec((d, tn), lambda i, j: (0, j))]
    out_spec = pl.BlockSpec((tm, tn), lambda i, j: (i, j))
    out_shape = jax.ShapeDtypeStruct((s, n), BF16)
    scratch = [pltpu.VMEM((tm, d), BF16)]
    cp = _params(("arbitrary", "arbitrary"), 48)
    if side is None:
        return pl.pallas_call(_nmm_kernel, grid=(s // tm, n // tn), in_specs=in_specs,
                              out_specs=out_spec, out_shape=out_shape, scratch_shapes=scratch,
                              compiler_params=cp, name="norm_matmul")(x, g, sc, sh, w)
    ws, bs = side
    ns = ws.shape[1]
    in_specs += [pl.BlockSpec((d, ns), row), pl.BlockSpec((1, ns), row)]
    return pl.pallas_call(
        _nmm_side_kernel, grid=(s // tm, n // tn), in_specs=in_specs,
        out_specs=[out_spec, pl.BlockSpec((tm, ns), lambda i, j: (i, 0))],
        out_shape=[out_shape, jax.ShapeDtypeStruct((s, ns), F32)],
        scratch_shapes=scratch, compiler_params=cp, name="norm_matmul_side",
    )(x, g, sc, sh, w, ws, bs)


POOL_HALO = 16


def _inpool_kernel(x_ref, g_ref, sc_ref, sh_ref, w_ref, o_ref, h_ref, u_ref, carry_ref, *,
                   cols_per_group):
    i = pl.program_id(0)
    j = pl.program_id(1)
    tm, tn = o_ref.shape

    @pl.when(j == 0)
    def _():
        _norm_rows_into(x_ref, g_ref, sc_ref, sh_ref, h_ref)

    u = _dot(h_ref[...], w_ref[...])

    @pl.when(i == 0)
    def _():
        u_ref[0:POOL_HALO, :] = jnp.zeros((POOL_HALO, tn), F32)

    @pl.when(i > 0)
    def _():
        u_ref[0:POOL_HALO, :] = carry_ref[j]

    u_ref[POOL_HALO:, :] = u
    carry_ref[j] = u[tm - POOL_HALO:, :]

    t1 = (i * tm + lax.broadcasted_iota(jnp.int32, (tm, 1), 0) + 1).astype(F32)
    group = (j * tn) // cols_per_group
    for gi, w in enumerate(POOL_WINDOWS):
        @pl.when(group == gi)
        def _(w=w):
            acc = u_ref[POOL_HALO:, :]
            for k in range(1, w):
                acc = acc + u_ref[POOL_HALO - k:POOL_HALO - k + tm, :]
            cnt = jnp.minimum(t1, float(w))
            o_ref[...] = (acc / cnt - u_ref[POOL_HALO:, :]).astype(o_ref.dtype)


def _in_proj_pool(x, g, sc, sh, w_in):
    s, d = x.shape
    n = w_in.shape[1]
    cg = n // len(POOL_WINDOWS)
    tm = _tile(s, 512, 16)
    tn = _tile(cg, 512, 128)
    row = lambda i, j: (0, 0)
    return pl.pallas_call(
        functools.partial(_inpool_kernel, cols_per_group=cg),
        grid=(s // tm, n // tn),
        in_specs=[pl.BlockSpec((tm, d), lambda i, j: (i, 0)),
                  pl.BlockSpec((1, d), row), pl.BlockSpec((1, d), row), pl.BlockSpec((1, d), row),
                  pl.BlockSpec((d, tn), lambda i, j: (0, j))],
        out_specs=pl.BlockSpec((tm, tn), lambda i, j: (i, j)),
        out_shape=jax.ShapeDtypeStruct((s, n), BF16),
        scratch_shapes=[pltpu.VMEM((tm, d), BF16),
                        pltpu.VMEM((POOL_HALO + tm, tn), F32),
                        pltpu.VMEM((n // tn, POOL_HALO, tn), F32)],
        compiler_params=_params(("arbitrary", "arbitrary"), 48),
        name="in_proj_pool",
    )(x, g, sc, sh, w_in)


def _poolout_kernel(p_ref, wg_ref, bg_ref, sg_ref, wo_ref, x_ref, gate_ref, o_ref, y_ref):
    @pl.when(pl.program_id(1) == 0)
    def _():
        n_groups, cg, _ = wg_ref.shape
        for gi in range(n_groups):
            cols = slice(gi * cg, (gi + 1) * cg)
            y = _dot(p_ref[:, cols], wg_ref[gi])
            y_ref[:, cols] = ((y + bg_ref[gi]) * sg_ref[gi]).astype(y_ref.dtype)

    o_ref[...] = x_ref[...] + gate_ref[...] * _dot(y_ref[...], wo_ref[...])


def _pool_out(p, w_grp, b_grp, scale, w_out, x, gate):
    s, d = x.shape
    ng, cg, _ = w_grp.shape
    tm = _tile(s, 512, 16)
    tn = _tile(d, 512, 128)
    const3 = lambda i, j: (0, 0, 0)
    return pl.pallas_call(
        _poolout_kernel,
        grid=(s // tm, d // tn),
        in_specs=[pl.BlockSpec((tm, d), lambda i, j: (i, 0)),
                  pl.BlockSpec((ng, cg, cg), const3),
                  pl.BlockSpec((ng, 1, cg), const3),
                  pl.BlockSpec((ng, 1, cg), const3),
                  pl.BlockSpec((d, tn), lambda i, j: (0, j)),
                  pl.BlockSpec((tm, tn), lambda i, j: (i, j)),
                  pl.BlockSpec((1, tn), lambda i, j: (0, j))],
        out_specs=pl.BlockSpec((tm, tn), lambda i, j: (i, j)),
        out_shape=jax.ShapeDtypeStruct((s, d), F32),
        scratch_shapes=[pltpu.VMEM((tm, d), BF16)],
        compiler_params=_params(("arbitrary", "arbitrary"), 52),
        name="pool_out",
    )(p, w_grp, b_grp.reshape(ng, 1, cg), scale.reshape(ng, 1, cg), w_out, x, gate)


LANES = 128
HIGH_HALF = 0xFFFF0000


def _pack_rows(v):
    half = v.shape[1] // 2
    lo = lax.bitcast_convert_type(v[:, :half].astype(BF16).astype(F32), jnp.uint32)
    hi = lax.bitcast_convert_type(v[:, half:].astype(BF16).astype(F32), jnp.uint32)
    return (lo >> 16) | (hi & jnp.uint32(HIGH_HALF))


def _unpack_lo(u):
    return lax.bitcast_convert_type(u << 16, F32)


def _unpack_hi(u):
    return lax.bitcast_convert_type(u & jnp.uint32(HIGH_HALF), F32)


def _slab(c, rows, sub):
    return pl.ds(c, rows, stride=sub)


def _top_k(logits):
    n_e = logits.shape[-1]
    lane = lax.broadcasted_iota(jnp.int32, logits.shape, 1)
    work = logits
    vals, idxs = [], []
    for _ in range(TOP_K):
        m = jnp.max(work, axis=-1, keepdims=True)
        idx = jnp.min(jnp.where(work == m, lane, n_e), axis=-1, keepdims=True)
        vals.append(m)
        idxs.append(idx)
        work = jnp.where(lane == idx, -jnp.inf, work)
    return vals, idxs


def _route_kernel(x_ref, g_ref, sc_ref, sh_ref, wr_ref, br_ref,
                  hp_ref, idx_ref, w_ref, pos_ref, cnt_ref, lg_ref, carry_ref):
    tm, d = x_ref.shape
    sub = hp_ref.shape[0] // tm
    n_e = wr_ref.shape[1]
    rows = min(NORM_ROWS, tm)

    @pl.when(pl.program_id(0) == 0)
    def _():
        carry_ref[...] = jnp.zeros(carry_ref.shape, F32)

    gs = g_ref[...] * (1.0 + sc_ref[...])
    sh = sh_ref[...]
    wr = wr_ref[...]
    wr_hi = wr.astype(BF16)
    wr_lo = (wr - wr_hi.astype(F32)).astype(BF16)
    for r in range(tm // rows):
        sl = slice(r * rows, (r + 1) * rows)
        h = _modulated_norm(x_ref[sl, :], gs, sh)
        h_hi = h.astype(BF16)
        h_lo = (h - h_hi.astype(F32)).astype(BF16)
        lg_ref[sl, :] = _dot(h_hi, wr_hi) + _dot(h_lo, wr_hi) + _dot(h_hi, wr_lo)
        packed = _pack_rows(h)
        for c in range(sub):
            hp_ref[pl.ds(r * rows * sub + c, rows, stride=sub), :] = packed[:, c * LANES:(c + 1) * LANES]

    vals, idxs = _top_k(lg_ref[...] + br_ref[...])
    ex = [jnp.exp(v - vals[0]) for v in vals]
    den = ex[0]
    for e in ex[1:]:
        den = den + e

    lane = lax.broadcasted_iota(jnp.int32, (tm, n_e), 1)
    sel = jnp.zeros((tm, n_e), F32)
    for idx in idxs:
        sel = jnp.where(lane == idx, 1.0, sel)
    r_i = lax.broadcasted_iota(jnp.int32, (tm, tm), 0)
    c_i = lax.broadcasted_iota(jnp.int32, (tm, tm), 1)
    below = jnp.where(c_i < r_i, 1.0, 0.0).astype(BF16)
    ranks = _dot(below, sel.astype(BF16)) + carry_ref[...]

    slot = lax.broadcasted_iota(jnp.int32, (tm, TOP_K), 1)
    idx_out = jnp.zeros((tm, TOP_K), jnp.int32)
    w_out = jnp.zeros((tm, TOP_K), F32)
    pos_out = jnp.zeros((tm, TOP_K), F32)
    for k in range(TOP_K):
        pk = jnp.sum(jnp.where(lane == idxs[k], ranks, 0.0), axis=-1, keepdims=True)
        idx_out = jnp.where(slot == k, idxs[k], idx_out)
        w_out = jnp.where(slot == k, ex[k] / den, w_out)
        pos_out = jnp.where(slot == k, pk, pos_out)
    idx_ref[...] = idx_out
    w_ref[...] = w_out
    pos_ref[...] = pos_out.astype(jnp.int32)
    carry_ref[...] += jnp.sum(sel, axis=0, keepdims=True)
    cnt_ref[...] = carry_ref[...]


def _route(x, g, sc, sh, w_router, b_router):
    s, d = x.shape
    n_e = w_router.shape[1]
    sub = d // 2 // LANES
    tm = _tile(s, 512, 16)
    row = lambda i: (0, 0)
    tok = lambda i: (i, 0)
    return pl.pallas_call(
        _route_kernel,
        grid=(s // tm,),
        in_specs=[pl.BlockSpec((tm, d), tok),
                  pl.BlockSpec((1, d), row), pl.BlockSpec((1, d), row), pl.BlockSpec((1, d), row),
                  pl.BlockSpec((d, n_e), row), pl.BlockSpec((1, n_e), row)],
        out_specs=[pl.BlockSpec((tm * sub, LANES), tok),
                   pl.BlockSpec((tm, TOP_K), tok), pl.BlockSpec((tm, TOP_K), tok),
                   pl.BlockSpec((tm, TOP_K), tok), pl.BlockSpec((1, n_e), row)],
        out_shape=[jax.ShapeDtypeStruct((s * sub, LANES), jnp.uint32),
                   jax.ShapeDtypeStruct((s, TOP_K), jnp.int32),
                   jax.ShapeDtypeStruct((s, TOP_K), F32),
                   jax.ShapeDtypeStruct((s, TOP_K), jnp.int32),
                   jax.ShapeDtypeStruct((1, n_e), F32)],
        scratch_shapes=[pltpu.VMEM((tm, n_e), F32), pltpu.VMEM((1, n_e), F32)],
        compiler_params=_params(("arbitrary",), 40),
        name="moe_route",
    )(x, g, sc, sh, w_router, b_router.reshape(1, n_e))


def _dispatch_kernel(dest_hbm, hp_ref, hs_hbm, dest_smem, idx_sem, row_sem, *, sub):
    n = dest_smem.shape[0]
    base = pl.multiple_of(pl.program_id(0) * n, n)
    idx_copy = pltpu.make_async_copy(dest_hbm.at[pl.ds(base, n)], dest_smem, idx_sem)
    idx_copy.start()
    idx_copy.wait()

    def row_copy(t, k):
        src = pl.multiple_of(t * sub, sub)
        dst = pl.multiple_of(dest_smem[t * TOP_K + k] * sub, sub)
        return pltpu.make_async_copy(hp_ref.at[pl.ds(src, sub), :], hs_hbm.at[pl.ds(dst, sub), :],
                                     row_sem)

    def issue(t, carry):
        for k in range(TOP_K):
            row_copy(t, k).start()
        return carry

    def drain(t, carry):
        for k in range(TOP_K):
            row_copy(t, k).wait()
        return carry

    lax.fori_loop(0, n // TOP_K, issue, 0, unroll=2)
    lax.fori_loop(0, n // TOP_K, drain, 0, unroll=2)


def _dispatch(hp, dest, sub):
    m = dest.shape[0]
    s = m // TOP_K
    tm = _tile(s, 256, 256)
    return pl.pallas_call(
        functools.partial(_dispatch_kernel, sub=sub),
        grid=(s // tm,),
        in_specs=[pl.BlockSpec(memory_space=pl.ANY),
                  pl.BlockSpec((tm * sub, LANES), lambda i: (i, 0))],
        out_specs=pl.BlockSpec(memory_space=pl.ANY),
        out_shape=jax.ShapeDtypeStruct((m * sub, LANES), hp.dtype),
        scratch_shapes=[pltpu.SMEM((tm * TOP_K,), jnp.int32),
                        pltpu.SemaphoreType.DMA, pltpu.SemaphoreType.DMA],
        compiler_params=_params(("arbitrary",), 16),
        name="moe_dispatch",
    )(dest, hp)


def _experts_kernel(vt_ref, ve_ref, lo_ref, hi_ref, hs_ref, wgu_ref, bgu_ref, wdn_ref, o_ref):
    v = pl.program_id(0)
    lo = lo_ref[v]
    hi = hi_ref[v]
    f, d = wdn_ref.shape
    half = d // 2
    sub = half // LANES
    tr = hs_ref.shape[0] // sub

    @pl.when(hi > lo)
    def _():
        hu = jnp.concatenate([hs_ref[_slab(c, tr, sub), :] for c in range(sub)], axis=1)
        h_lo = _unpack_lo(hu).astype(BF16)
        h_hi = _unpack_hi(hu).astype(BF16)
        gu = _dot(h_lo, wgu_ref[:half, :]) + _dot(h_hi, wgu_ref[half:, :]) + bgu_ref[...]
        gt = jnp.minimum(gu[:, :f], SWIGLU_LIMIT)
        ln = jnp.clip(gu[:, f:], -SWIGLU_LIMIT, SWIGLU_LIMIT)
        act = (ln + 1.0) * gt * jax.nn.sigmoid(SWIGLU_ALPHA * gt)
        packed = _pack_rows(_dot(act.astype(BF16), wdn_ref[...]))
        row = lax.broadcasted_iota(jnp.int32, (tr, 1), 0)
        mine = (row >= lo) & (row < hi)

        @pl.when(lo == 0)
        def _():
            for c in range(sub):
                o_ref[_slab(c, tr, sub), :] = jnp.where(mine, packed[:, c * LANES:(c + 1) * LANES],
                                                        jnp.uint32(0))

        @pl.when(lo > 0)
        def _():
            for c in range(sub):
                rows = _slab(c, tr, sub)
                o_ref[rows, :] = jnp.where(mine, packed[:, c * LANES:(c + 1) * LANES], o_ref[rows, :])


EXPERT_TILE = 256


def _experts(hs, visits, w_gu, b_gu, w_dn, tr, sub):
    vt, ve, vlo, vhi = visits
    n_e, d, f2 = w_gu.shape
    f = f2 // 2
    grid_spec = pltpu.PrefetchScalarGridSpec(
        num_scalar_prefetch=4,
        grid=(vt.shape[0],),
        in_specs=[pl.BlockSpec((tr * sub, LANES), lambda v, vt, ve, lo, hi: (vt[v], 0)),
                  pl.BlockSpec((None, d, f2), lambda v, vt, ve, lo, hi: (ve[v], 0, 0)),
                  pl.BlockSpec((None, 1, f2), lambda v, vt, ve, lo, hi: (ve[v], 0, 0)),
                  pl.BlockSpec((None, f, d), lambda v, vt, ve, lo, hi: (ve[v], 0, 0))],
        out_specs=pl.BlockSpec((tr * sub, LANES), lambda v, vt, ve, lo, hi: (vt[v], 0)),
    )
    return pl.pallas_call(
        _experts_kernel, grid_spec=grid_spec,
        out_shape=jax.ShapeDtypeStruct(hs.shape, jnp.uint32),
        compiler_params=_params(("arbitrary",), 48),
        name="moe_experts",
    )(vt, ve, vlo, vhi, hs, w_gu, b_gu.reshape(n_e, 1, f2), w_dn)


def _combine_kernel(dest_hbm, ys_hbm, w_ref, idx_ref, bd_ref, x_ref, gate_ref,
                    o_ref, bias_ref, buf_ref, dest_smem, idx_sem, row_sem):
    i = pl.program_id(0)
    tm, d = x_ref.shape
    half = d // 2
    sub = half // LANES
    n_e = bd_ref.shape[0]
    n = tm * TOP_K
    slot = i % 2

    def fetch_indices(tile, sl):
        base = pl.multiple_of(tile * n, n)
        copy = pltpu.make_async_copy(dest_hbm.at[pl.ds(base, n)], dest_smem.at[sl], idx_sem)
        copy.start()
        copy.wait()

    def row_copy(sl, t, k):
        src = pl.multiple_of(dest_smem[sl, t * TOP_K + k] * sub, sub)
        dst = pl.multiple_of(t * sub, sub)
        return pltpu.make_async_copy(ys_hbm.at[pl.ds(src, sub), :],
                                     buf_ref.at[sl, k, pl.ds(dst, sub), :], row_sem.at[sl])

    def request_rows(sl):
        def issue(t, carry):
            for k in range(TOP_K):
                row_copy(sl, t, k).start()
            return carry
        lax.fori_loop(0, tm, issue, 0, unroll=2)

    @pl.when(i == 0)
    def _():
        fetch_indices(0, 0)
        request_rows(0)

    @pl.when(i + 1 < pl.num_programs(0))
    def _():
        fetch_indices(i + 1, 1 - slot)
        request_rows(1 - slot)

    w = w_ref[...]
    idx = idx_ref[...]
    lane = lax.broadcasted_iota(jnp.int32, (tm, n_e), 1)
    comb = jnp.zeros((tm, n_e), F32)
    for k in range(TOP_K):
        comb = comb + jnp.where(lane == idx[:, k:k + 1], w[:, k:k + 1], 0.0)
    bias_ref[...] = _dot_f32ish(comb, bd_ref[...])

    def drain(t, carry):
        for k in range(TOP_K):
            row_copy(slot, t, k).wait()
        return carry
    lax.fori_loop(0, tm, drain, 0, unroll=2)

    for c in range(sub):
        lo = jnp.zeros((tm, LANES), F32)
        hi = jnp.zeros((tm, LANES), F32)
        for k in range(TOP_K):
            u = buf_ref[slot, k, _slab(c, tm, sub), :]
            lo = lo + w[:, k:k + 1] * _unpack_lo(u)
            hi = hi + w[:, k:k + 1] * _unpack_hi(u)
        cl = slice(c * LANES, (c + 1) * LANES)
        ch = slice(half + c * LANES, half + (c + 1) * LANES)
        o_ref[:, cl] = x_ref[:, cl] + gate_ref[:, cl] * (lo + bias_ref[:, cl])
        o_ref[:, ch] = x_ref[:, ch] + gate_ref[:, ch] * (hi + bias_ref[:, ch])


def _combine(ys, dest, w4, idx4, b_dn, x, gate):
    s, d = x.shape
    n_e = b_dn.shape[0]
    sub = d // 2 // LANES
    tm = _tile(s, 256, 256)
    tok = lambda i: (i, 0)
    row = lambda i: (0, 0)
    any_spec = pl.BlockSpec(memory_space=pl.ANY)
    return pl.pallas_call(
        _combine_kernel,
        grid=(s // tm,),
        in_specs=[any_spec, any_spec,
                  pl.BlockSpec((tm, TOP_K), tok), pl.BlockSpec((tm, TOP_K), tok),
                  pl.BlockSpec((n_e, d), row), pl.BlockSpec((tm, d), tok),
                  pl.BlockSpec((1, d), row)],
        out_specs=pl.BlockSpec((tm, d), tok),
        out_shape=jax.ShapeDtypeStruct((s, d), F32),
        scratch_shapes=[pltpu.VMEM((tm, d), F32),
                        pltpu.VMEM((2, TOP_K, tm * sub, LANES), jnp.uint32),
                        pltpu.SMEM((2, tm * TOP_K), jnp.int32),
                        pltpu.SemaphoreType.DMA, pltpu.SemaphoreType.DMA((2,))],
        compiler_params=_params(("arbitrary",), 52),
        name="moe_combine",
    )(dest, ys, w4, idx4, b_dn, x, gate)


def _expert_visits(off, m, tr, n_e):
    nt = m // tr
    pts = jnp.sort(jnp.concatenate([jnp.arange(nt, dtype=jnp.int32) * tr, off[:n_e]]))
    nxt = jnp.concatenate([pts[1:], jnp.full((1,), m, jnp.int32)])
    vt = jnp.minimum(pts // tr, nt - 1)
    ve = jnp.clip(jnp.searchsorted(off[1:], pts, side="right"), 0, n_e - 1).astype(jnp.int32)
    return vt, ve, pts - vt * tr, nxt - vt * tr


def _moe(x, g, sc, sh, gate, w_router, b_router, w_gu, b_gu, w_dn, b_dn):
    s, d = x.shape
    n_e = w_router.shape[1]
    sub = d // 2 // LANES
    m = TOP_K * s
    tr = _tile(m, EXPERT_TILE, 16)
    hp, idx4, w4, pos4, counts = _route(x, g, sc, sh, w_router, b_router)
    counts = counts.reshape(n_e).astype(jnp.int32)
    off = jnp.concatenate([jnp.zeros((1,), jnp.int32), jnp.cumsum(counts)])
    dest = (off[idx4] + pos4).reshape(m)
    hs = _dispatch(hp, dest, sub)
    ys = _experts(hs, _expert_visits(off, m, tr, n_e), w_gu, b_gu, w_dn, tr, sub)
    return _combine(ys, dest, w4, idx4, b_dn, x, gate)


def _prep_gu_kernel(w_ref, o_ref):
    n = w_ref.shape[1]
    src = lax.broadcasted_iota(jnp.int32, (n, n), 0)
    dst = lax.broadcasted_iota(jnp.int32, (n, n), 1)
    want = jnp.where(dst < n // 2, 2 * dst, 2 * (dst - n // 2) + 1)
    perm = jnp.where(src == want, 1.0, 0.0).astype(BF16)
    o_ref[...] = _dot(w_ref[...].astype(BF16), perm).astype(BF16)


def _prep_gu(w_gu):
    shape = w_gu.shape
    n = shape[-1]
    rows = w_gu.size // n
    tm = _tile(rows, 2048, 16)
    out = pl.pallas_call(
        _prep_gu_kernel, grid=(rows // tm,),
        in_specs=[pl.BlockSpec((tm, n), lambda i: (i, 0))],
        out_specs=pl.BlockSpec((tm, n), lambda i: (i, 0)),
        out_shape=jax.ShapeDtypeStruct((rows, n), BF16),
        compiler_params=_params(("arbitrary",), 32),
        name="prep_expert_up",
    )(w_gu.reshape(rows, n))
    return out.reshape(shape)


def _fcum_kernel(fl_ref, o_ref, carry_ref):
    @pl.when(pl.program_id(0) == 0)
    def _():
        carry_ref[...] = jnp.zeros(carry_ref.shape, F32)

    x = fl_ref[...]
    log_f = jnp.minimum(x, 0.0) - jnp.log1p(jnp.exp(-jnp.abs(x)))
    tm = x.shape[0]
    r = lax.broadcasted_iota(jnp.int32, (tm, tm), 0)
    c = lax.broadcasted_iota(jnp.int32, (tm, tm), 1)
    tri = jnp.where(c <= r, 1.0, 0.0).astype(BF16)
    hi, mid, lo = _split3(log_f)
    out = (_dot(tri, hi) + _dot(tri, mid) + _dot(tri, lo)) + carry_ref[...]
    o_ref[...] = out
    carry_ref[...] = out[tm - 1:tm, :]


def _forget_cumsum(fl):
    s, h = fl.shape
    tm = _tile(s, 256, 8)
    return pl.pallas_call(
        _fcum_kernel, grid=(s // tm,),
        in_specs=[pl.BlockSpec((tm, h), lambda i: (i, 0))],
        out_specs=pl.BlockSpec((tm, h), lambda i: (i, 0)),
        out_shape=jax.ShapeDtypeStruct((s, h), F32),
        scratch_shapes=[pltpu.VMEM((1, h), F32)],
        compiler_params=_params(("arbitrary",), 16),
        name="forget_cumsum",
    )(fl)


def _aug_kernel(src_ref, f_ref, g_ref, o_ref, *, mult, is_query):
    tm = src_ref.shape[0]
    n_heads = f_ref.shape[1]
    gm = g_ref[...] * mult
    lane = lax.broadcasted_iota(jnp.int32, (tm, HEAD_DIM), 1)
    for h in range(n_heads):
        t = src_ref[:, h * HEAD_DIM:(h + 1) * HEAD_DIM].astype(F32)
        ms = jnp.mean(t * t, axis=-1, keepdims=True)
        o_ref[:, h * AUG_DIM:h * AUG_DIM + HEAD_DIM] = (t * lax.rsqrt(ms + EPS) * gm).astype(BF16)
        f = jnp.broadcast_to(f_ref[:, h:h + 1] * LOG2E, (tm, HEAD_DIM))
        hi, mid, lo = [p.astype(F32) for p in _split3(f)]
        one = jnp.ones((tm, HEAD_DIM), F32)
        if is_query:
            pieces = (hi, mid, lo, one, one, one)
        else:
            pieces = (one, one, one, -hi, -mid, -lo)
        extra = jnp.zeros((tm, HEAD_DIM), F32)
        for p, piece in enumerate(pieces):
            extra = jnp.where(lane == p, piece, extra)
        o_ref[:, h * AUG_DIM + HEAD_DIM:(h + 1) * AUG_DIM] = extra.astype(BF16)


def _augment(src, f_cum, g, *, mult, is_query):
    s = src.shape[0]
    n_heads = f_cum.shape[1]
    d = n_heads * HEAD_DIM
    tm = _tile(s, 256, 16)
    return pl.pallas_call(
        functools.partial(_aug_kernel, mult=mult, is_query=is_query),
        grid=(s // tm,),
        in_specs=[pl.BlockSpec((tm, d), lambda i: (i, 0)),
                  pl.BlockSpec((tm, n_heads), lambda i: (i, 0)),
                  pl.BlockSpec((1, HEAD_DIM), lambda i: (0, 0))],
        out_specs=pl.BlockSpec((tm, n_heads * AUG_DIM), lambda i: (i, 0)),
        out_shape=jax.ShapeDtypeStruct((s, n_heads * AUG_DIM), BF16),
        compiler_params=_params(("arbitrary",), 32),
        name="augment_q" if is_query else "augment_k",
    )(src, f_cum, g.reshape(1, HEAD_DIM))


def _attn_kernel(start_ref, q_ref, k_ref, v_ref, o_ref, m_ref, l_ref, acc_ref, *, tk):
    h = pl.program_id(0)
    qi = pl.program_id(1)
    n_sub = q_ref.shape[0] // tk
    m_ref[...] = jnp.full(m_ref.shape, MASK_VALUE, F32)
    l_ref[...] = jnp.zeros(l_ref.shape, F32)
    acc_ref[...] = jnp.zeros(acc_ref.shape, F32)

    def sub_step(a, kb, on_diagonal):
        qrows = slice(a * tk, (a + 1) * tk)
        krows = pl.ds(pl.multiple_of(kb * tk, tk), tk)
        s = lax.dot_general(q_ref[qrows, :], k_ref[krows, :], (((1,), (1,)), ((), ())),
                            preferred_element_type=F32)
        if on_diagonal:
            r = lax.broadcasted_iota(jnp.int32, s.shape, 0)
            c = lax.broadcasted_iota(jnp.int32, s.shape, 1)
            s = jnp.where(c <= r, s, MASK_VALUE)
        m_prev = m_ref[qrows, :]
        m_new = jnp.maximum(m_prev, jnp.max(s, axis=-1, keepdims=True))
        alpha = jnp.exp2(m_prev - m_new)
        p = jnp.exp2(s - m_new)
        l_ref[qrows, :] = alpha * l_ref[qrows, :] + jnp.sum(p, axis=-1, keepdims=True)
        acc_ref[qrows, :] = alpha * acc_ref[qrows, :] + _dot(p.astype(BF16), v_ref[krows, :])
        m_ref[qrows, :] = m_new

    def body(kb, carry):
        for a in range(n_sub):
            sub_step(a, kb, False)
        return carry

    first_diag = qi * n_sub
    lax.fori_loop(start_ref[h, qi], first_diag, body, 0)
    for a in range(n_sub):
        for b in range(a):
            sub_step(a, first_diag + b, False)
        sub_step(a, first_diag + a, True)
    o_ref[...] = (acc_ref[...] / l_ref[...]).astype(o_ref.dtype)


def _attention(q_aug, k_aug, kv, start, n_heads, tq, tk):
    s = q_aug.shape[0]
    nq = s // tq
    grid_spec = pltpu.PrefetchScalarGridSpec(
        num_scalar_prefetch=1,
        grid=(n_heads, nq),
        in_specs=[pl.BlockSpec((tq, AUG_DIM), lambda h, qi, st: (qi, h)),
                  pl.BlockSpec((s, AUG_DIM), lambda h, qi, st: (0, h)),
                  pl.BlockSpec((s, HEAD_DIM), lambda h, qi, st: (0, n_heads + h))],
        out_specs=pl.BlockSpec((tq, HEAD_DIM), lambda h, qi, st: (qi, h)),
        scratch_shapes=[pltpu.VMEM((tq, 1), F32), pltpu.VMEM((tq, 1), F32),
                        pltpu.VMEM((tq, HEAD_DIM), F32)],
    )
    return pl.pallas_call(
        functools.partial(_attn_kernel, tk=tk), grid_spec=grid_spec,
        out_shape=jax.ShapeDtypeStruct((s, n_heads * HEAD_DIM), BF16),
        compiler_params=_params(("arbitrary", "arbitrary"), 48),
        name="fox_attention",
    )(start, q_aug, k_aug, kv)


def _attention_start_blocks(f_cum, q_norm_g, k_norm_g, tq, tk):
    s, n_heads = f_cum.shape
    nq = s // tq
    nk = s // tk
    bound = HEAD_DIM ** 0.5 * jnp.max(jnp.abs(q_norm_g * k_norm_g))
    thresh = F32_EXP_UNDERFLOW + 2.0 * bound + 1.0
    f_max_q = jnp.max(f_cum.reshape(nq, tq, n_heads), axis=1).T
    f_min_k = jnp.min(f_cum.reshape(nk, tk, n_heads), axis=1).T
    needed = (f_max_q[:, :, None] - f_min_k[:, None, :]) >= -thresh
    first_diag = jnp.arange(nq) * (tq // tk)
    needed = needed | (jnp.arange(nk)[None, None, :] >= first_diag[None, :, None])
    return jnp.argmax(needed, axis=-1).astype(jnp.int32)


def _gateout_kernel(o_ref, gp_ref, wo_ref, x_ref, gate_ref, out_ref, a_ref):
    @pl.when(pl.program_id(1) == 0)
    def _():
        a = o_ref[...].astype(F32) * jax.nn.sigmoid(gp_ref[...].astype(F32))
        a_ref[...] = a.astype(a_ref.dtype)

    out_ref[...] = x_ref[...] + gate_ref[...] * _dot(a_ref[...], wo_ref[...])


def _gate_out(o, qg, w_o, x, gate):
    s, d = x.shape
    tm = _tile(s, 512, 16)
    tn = _tile(d, 512, 128)
    return pl.pallas_call(
        _gateout_kernel,
        grid=(s // tm, d // tn),
        in_specs=[pl.BlockSpec((tm, d), lambda i, j: (i, 0)),
                  pl.BlockSpec((tm, d), lambda i, j: (i, 1)),
                  pl.BlockSpec((d, tn), lambda i, j: (0, j)),
                  pl.BlockSpec((tm, tn), lambda i, j: (i, j)),
                  pl.BlockSpec((1, tn), lambda i, j: (0, j))],
        out_specs=pl.BlockSpec((tm, tn), lambda i, j: (i, j)),
        out_shape=jax.ShapeDtypeStruct((s, d), F32),
        scratch_shapes=[pltpu.VMEM((tm, d), BF16)],
        compiler_params=_params(("arbitrary", "arbitrary"), 48),
        name="gate_out",
    )(o, qg, w_o, x, gate)


ATTN_KV_BLOCK = 512
ATTN_Q_SUBTILES = 2


def kernel(x, c, w_ada, b_ada, ada_table, norm1_g, norm2_g, a_w_in, a_w_grp, a_b_grp, a_scale,
           a_w_out, kv_norm_g, w_kvf, b_f, k_norm_g, b_w_qg, q_norm_g, b_w_o, moe_w_router,
           moe_b_router, moe_w_gu, moe_b_gu, moe_w_dn, moe_b_dn):
    b, s, d = x.shape
    assert b == 1, "single-sequence trunk"
    depth = ada_table.shape[0]
    n_mod = ada_table.shape[1]
    n_a = a_w_in.shape[0]
    n_heads = d // HEAD_DIM
    tk = _tile(s, ATTN_KV_BLOCK, 16)
    tq = tk * ATTN_Q_SUBTILES if s % (tk * ATTN_Q_SUBTILES) == 0 else tk
    xs = x.reshape(s, d)

    mod = _ada(c, w_ada, b_ada).reshape(n_mod, d)
    zeros_row = jnp.zeros((1, d), F32)
    w_gu_all = _prep_gu(moe_w_gu)
    b_gu_all = jnp.concatenate([moe_b_gu[..., 0::2], moe_b_gu[..., 1::2]], axis=-1)
    k_aug = kv = f_cum = None

    for layer in range(depth):
        m = mod + ada_table[layer]
        shift1, scale1, gate1, shift2, scale2, gate2 = [m[i:i + 1] for i in range(n_mod)]
        g1 = norm1_g[layer].reshape(1, d)
        g2 = norm2_g[layer].reshape(1, d)
        if layer < n_a:
            p = _in_proj_pool(xs, g1, scale1, shift1, a_w_in[layer].astype(BF16))
            xs = _pool_out(p, a_w_grp[layer].astype(BF16), a_b_grp[layer], a_scale[layer],
                           a_w_out[layer].astype(BF16), xs, gate1)
        else:
            j = layer - n_a
            qg = _norm_matmul(xs, g1, scale1, shift1, b_w_qg[j].astype(BF16))
            q_aug = _augment(qg, f_cum, q_norm_g[j], mult=HEAD_DIM ** -0.5 * LOG2E, is_query=True)
            start = _attention_start_blocks(f_cum, q_norm_g[j], k_norm_g, tq, tk)
            o = _attention(q_aug, k_aug, kv, start, n_heads, tq, tk)
            xs = _gate_out(o, qg, b_w_o[j].astype(BF16), xs, gate1)

        xs = _moe(xs, g2, scale2, shift2, gate2, moe_w_router[layer], moe_b_router[layer],
                  w_gu_all[layer], b_gu_all[layer], moe_w_dn[layer].astype(BF16), moe_b_dn[layer])

        if layer == n_a - 1:
            kv, fl = _norm_matmul(xs, kv_norm_g.reshape(1, d), zeros_row, zeros_row,
                                  w_kvf[:, :2 * d].astype(BF16),
                                  side=(w_kvf[:, 2 * d:].astype(BF16), b_f.reshape(1, n_heads)))
            f_cum = _forget_cumsum(fl)
            k_aug = _augment(kv, f_cum, k_norm_g, mult=1.0, is_query=False)

    return xs.reshape(b, s, d)
```

```python
import functools

import jax
import jax.numpy as jnp
from jax import lax
from jax.experimental import pallas as pl
from jax.experimental.pallas import tpu as pltpu

F32 = jnp.float32
BF16 = jnp.bfloat16

HEAD_DIM = 128
POOL_WINDOWS = (2, 4, 8, 16)
TOP_K = 4
SWIGLU_LIMIT = 7.0
SWIGLU_ALPHA = 1.702
EPS = 1e-6
LOG2E = 1.4426950408889634

AUG_DIM = 2 * HEAD_DIM
F32_EXP_UNDERFLOW = 88.0
MASK_VALUE = -1e30
MIB = 1 << 20


def _tile(n, pref, align):
    if n <= pref:
        return n
    t = (pref // align) * align
    while t >= align:
        if n % t == 0:
            return t
        t -= align
    raise ValueError(f"no tile for {n} (pref {pref}, align {align})")


def _params(semantics, vmem_mib):
    return pltpu.CompilerParams(dimension_semantics=semantics, vmem_limit_bytes=vmem_mib * MIB)


def _split3(a):
    hi = a.astype(BF16)
    r1 = a - hi.astype(F32)
    mid = r1.astype(BF16)
    lo = (r1 - mid.astype(F32)).astype(BF16)
    return hi, mid, lo


def _dot(a, b):
    return jnp.dot(a, b, preferred_element_type=F32)


def _dot_f32ish(a, b):
    ah = a.astype(BF16)
    al = (a - ah.astype(F32)).astype(BF16)
    bh = b.astype(BF16)
    bl = (b - bh.astype(F32)).astype(BF16)
    return _dot(ah, bh) + _dot(al, bh) + _dot(ah, bl)


def _modulated_norm(x, gs, sh):
    ms = jnp.mean(x * x, axis=-1, keepdims=True)
    return x * lax.rsqrt(ms + EPS) * gs + sh


NORM_ROWS = 64
WIDE_TN = 1024


def _norm_rows_into(x_ref, g_ref, sc_ref, sh_ref, h_ref):
    tm = x_ref.shape[0]
    rows = min(NORM_ROWS, tm)
    gs = g_ref[...] * (1.0 + sc_ref[...])
    sh = sh_ref[...]

    def body(r, carry):
        sl = pl.ds(pl.multiple_of(r * rows, rows), rows)
        h_ref[sl, :] = _modulated_norm(x_ref[sl, :], gs, sh).astype(h_ref.dtype)
        return carry

    lax.fori_loop(0, tm // rows, body, 0)


def _ada_kernel(c_ref, w_ref, b_ref, o_ref):
    c = c_ref[...]
    s = c * jax.nn.sigmoid(c)
    o_ref[...] = jnp.sum(w_ref[...] * s, axis=0, keepdims=True) + b_ref[...]


def _ada(c, w_ada, b_ada):
    d, n = w_ada.shape
    tn = _tile(n, 512, 128)
    return pl.pallas_call(
        _ada_kernel,
        grid=(n // tn,),
        in_specs=[pl.BlockSpec((d, 1), lambda j: (0, 0)),
                  pl.BlockSpec((d, tn), lambda j: (0, j)),
                  pl.BlockSpec((1, tn), lambda j: (0, j))],
        out_specs=pl.BlockSpec((1, tn), lambda j: (0, j)),
        out_shape=jax.ShapeDtypeStruct((1, n), F32),
        compiler_params=_params(("arbitrary",), 40),
        name="ada_proj",
    )(c.reshape(d, 1), w_ada, b_ada.reshape(1, n))


def _nmm_kernel(x_ref, g_ref, sc_ref, sh_ref, w_ref, o_ref, h_ref):
    @pl.when(pl.program_id(1) == 0)
    def _():
        _norm_rows_into(x_ref, g_ref, sc_ref, sh_ref, h_ref)

    o_ref[...] = _dot(h_ref[...], w_ref[...]).astype(o_ref.dtype)


def _nmm_side_kernel(x_ref, g_ref, sc_ref, sh_ref, w_ref, ws_ref, bs_ref, o_ref, os_ref, h_ref):
    @pl.when(pl.program_id(1) == 0)
    def _():
        _norm_rows_into(x_ref, g_ref, sc_ref, sh_ref, h_ref)
        os_ref[...] = _dot(h_ref[...], ws_ref[...]) + bs_ref[...]

    o_ref[...] = _dot(h_ref[...], w_ref[...]).astype(o_ref.dtype)


def _norm_matmul(x, g, sc, sh, w, side=None):
    s, d = x.shape
    n = w.shape[1]
    tm = _tile(s, 512, 16)
    tn = _tile(n, WIDE_TN, 128)
    row = lambda i, j: (0, 0)
    in_specs = [pl.BlockSpec((tm, d), lambda i, j: (i, 0)),
                pl.BlockSpec((1, d), row), pl.BlockSpec((1, d), row), pl.BlockSpec((1, d), row),
                pl.BlockSpec((d, tn), lambda i, j: (0, j))]
    out_spec = pl.BlockSpec((tm, tn), lambda i, j: (i, j))
    out_shape = jax.ShapeDtypeStruct((s, n), BF16)
    scratch = [pltpu.VMEM((tm, d), BF16)]
    cp = _params(("arbitrary", "arbitrary"), 48)
    if side is None:
        return pl.pallas_call(_nmm_kernel, grid=(s // tm, n // tn), in_specs=in_specs,
                              out_specs=out_spec, out_shape=out_shape, scratch_shapes=scratch,
                              compiler_params=cp, name="norm_matmul")(x, g, sc, sh, w)
    ws, bs = side
    ns = ws.shape[1]
    in_specs += [pl.BlockSpec((d, ns), row), pl.BlockSpec((1, ns), row)]
    return pl.pallas_call(
        _nmm_side_kernel, grid=(s // tm, n // tn), in_specs=in_specs,
        out_specs=[out_spec, pl.BlockSpec((tm, ns), lambda i, j: (i, 0))],
        out_shape=[out_shape, jax.ShapeDtypeStruct((s, ns), F32)],
        scratch_shapes=scratch, compiler_params=cp, name="norm_matmul_side",
    )(x, g, sc, sh, w, ws, bs)


POOL_HALO = 16


def _inpool_kernel(x_ref, g_ref, sc_ref, sh_ref, w_ref, o_ref, h_ref, u_ref, carry_ref, *,
                   cols_per_group):
    i = pl.program_id(0)
    j = pl.program_id(1)
    tm, tn = o_ref.shape

    @pl.when(j == 0)
    def _():
        _norm_rows_into(x_ref, g_ref, sc_ref, sh_ref, h_ref)

    u = _dot(h_ref[...], w_ref[...])

    @pl.when(i == 0)
    def _():
        u_ref[0:POOL_HALO, :] = jnp.zeros((POOL_HALO, tn), F32)

    @pl.when(i > 0)
    def _():
        u_ref[0:POOL_HALO, :] = carry_ref[j]

    u_ref[POOL_HALO:, :] = u
    carry_ref[j] = u[tm - POOL_HALO:, :]

    t1 = (i * tm + lax.broadcasted_iota(jnp.int32, (tm, 1), 0) + 1).astype(F32)
    group = (j * tn) // cols_per_group
    for gi, w in enumerate(POOL_WINDOWS):
        @pl.when(group == gi)
        def _(w=w):
            acc = u_ref[POOL_HALO:, :]
            for k in range(1, w):
                acc = acc + u_ref[POOL_HALO - k:POOL_HALO - k + tm, :]
            cnt = jnp.minimum(t1, float(w))
            o_ref[...] = (acc / cnt - u_ref[POOL_HALO:, :]).astype(o_ref.dtype)


def _in_proj_pool(x, g, sc, sh, w_in):
    s, d = x.shape
    n = w_in.shape[1]
    cg = n // len(POOL_WINDOWS)
    tm = _tile(s, 512, 16)
    tn = _tile(cg, WIDE_TN, 128)
    row = lambda i, j: (0, 0)
    return pl.pallas_call(
        functools.partial(_inpool_kernel, cols_per_group=cg),
        grid=(s // tm, n // tn),
        in_specs=[pl.BlockSpec((tm, d), lambda i, j: (i, 0)),
                  pl.BlockSpec((1, d), row), pl.BlockSpec((1, d), row), pl.BlockSpec((1, d), row),
                  pl.BlockSpec((d, tn), lambda i, j: (0, j))],
        out_specs=pl.BlockSpec((tm, tn), lambda i, j: (i, j)),
        out_shape=jax.ShapeDtypeStruct((s, n), BF16),
        scratch_shapes=[pltpu.VMEM((tm, d), BF16),
                        pltpu.VMEM((POOL_HALO + tm, tn), F32),
                        pltpu.VMEM((n // tn, POOL_HALO, tn), F32)],
        compiler_params=_params(("arbitrary", "arbitrary"), 48),
        name="in_proj_pool",
    )(x, g, sc, sh, w_in)


def _poolout_kernel(p_ref, wg_ref, bg_ref, sg_ref, wo_ref, x_ref, gate_ref, o_ref, y_ref):
    @pl.when(pl.program_id(1) == 0)
    def _():
        n_groups, cg, _ = wg_ref.shape
        for gi in range(n_groups):
            cols = slice(gi * cg, (gi + 1) * cg)
            y = _dot(p_ref[:, cols], wg_ref[gi])
            y_ref[:, cols] = ((y + bg_ref[gi]) * sg_ref[gi]).astype(y_ref.dtype)

    o_ref[...] = x_ref[...] + gate_ref[...] * _dot(y_ref[...], wo_ref[...])


def _pool_out(p, w_grp, b_grp, scale, w_out, x, gate):
    s, d = x.shape
    ng, cg, _ = w_grp.shape
    tm = _tile(s, 512, 16)
    tn = _tile(d, 512, 128)
    const3 = lambda i, j: (0, 0, 0)
    return pl.pallas_call(
        _poolout_kernel,
        grid=(s // tm, d // tn),
        in_specs=[pl.BlockSpec((tm, d), lambda i, j: (i, 0)),
                  pl.BlockSpec((ng, cg, cg), const3),
                  pl.BlockSpec((ng, 1, cg), const3),
                  pl.BlockSpec((ng, 1, cg), const3),
                  pl.BlockSpec((d, tn), lambda i, j: (0, j)),
                  pl.BlockSpec((tm, tn), lambda i, j: (i, j)),
                  pl.BlockSpec((1, tn), lambda i, j: (0, j))],
        out_specs=pl.BlockSpec((tm, tn), lambda i, j: (i, j)),
        out_shape=jax.ShapeDtypeStruct((s, d), F32),
        scratch_shapes=[pltpu.VMEM((tm, d), BF16)],
        compiler_params=_params(("arbitrary", "arbitrary"), 52),
        name="pool_out",
    )(p, w_grp, b_grp.reshape(ng, 1, cg), scale.reshape(ng, 1, cg), w_out, x, gate)


LANES = 128
HIGH_HALF = 0xFFFF0000


def _pack_rows(v):
    half = v.shape[1] // 2
    lo = lax.bitcast_convert_type(v[:, :half].astype(BF16).astype(F32), jnp.uint32)
    hi = lax.bitcast_convert_type(v[:, half:].astype(BF16).astype(F32), jnp.uint32)
    return (lo >> 16) | (hi & jnp.uint32(HIGH_HALF))


def _unpack_lo(u):
    return lax.bitcast_convert_type(u << 16, F32)


def _unpack_hi(u):
    return lax.bitcast_convert_type(u & jnp.uint32(HIGH_HALF), F32)


def _top_k(logits):
    n_e = logits.shape[-1]
    lane = lax.broadcasted_iota(jnp.int32, logits.shape, 1)
    work = logits
    vals, idxs = [], []
    for _ in range(TOP_K):
        m = jnp.max(work, axis=-1, keepdims=True)
        idx = jnp.min(jnp.where(work == m, lane, n_e), axis=-1, keepdims=True)
        vals.append(m)
        idxs.append(idx)
        work = jnp.where(lane == idx, -jnp.inf, work)
    return vals, idxs


def _route_kernel(x_ref, g_ref, sc_ref, sh_ref, wr_ref, br_ref,
                  hp_ref, idx_ref, w_ref, pos_ref, cnt_ref, lg_ref, carry_ref):
    tm, d = x_ref.shape
    n_e = wr_ref.shape[1]
    rows = min(NORM_ROWS, tm)

    @pl.when(pl.program_id(0) == 0)
    def _():
        carry_ref[...] = jnp.zeros(carry_ref.shape, F32)

    gs = g_ref[...] * (1.0 + sc_ref[...])
    sh = sh_ref[...]
    wr = wr_ref[...]
    wr_hi = wr.astype(BF16)
    wr_lo = (wr - wr_hi.astype(F32)).astype(BF16)
    for r in range(tm // rows):
        sl = slice(r * rows, (r + 1) * rows)
        h = _modulated_norm(x_ref[sl, :], gs, sh)
        h_hi = h.astype(BF16)
        h_lo = (h - h_hi.astype(F32)).astype(BF16)
        lg_ref[sl, :] = _dot(h_hi, wr_hi) + _dot(h_lo, wr_hi) + _dot(h_hi, wr_lo)
        hp_ref[sl, :] = _pack_rows(h)

    vals, idxs = _top_k(lg_ref[...] + br_ref[...])
    ex = [jnp.exp(v - vals[0]) for v in vals]
    den = ex[0]
    for e in ex[1:]:
        den = den + e

    lane = lax.broadcasted_iota(jnp.int32, (tm, n_e), 1)
    sel = jnp.zeros((tm, n_e), F32)
    for idx in idxs:
        sel = jnp.where(lane == idx, 1.0, sel)
    r_i = lax.broadcasted_iota(jnp.int32, (tm, tm), 0)
    c_i = lax.broadcasted_iota(jnp.int32, (tm, tm), 1)
    below = jnp.where(c_i < r_i, 1.0, 0.0).astype(BF16)
    ranks = _dot(below, sel.astype(BF16)) + carry_ref[...]

    slot = lax.broadcasted_iota(jnp.int32, (tm, TOP_K), 1)
    idx_out = jnp.zeros((tm, TOP_K), jnp.int32)
    w_out = jnp.zeros((tm, TOP_K), F32)
    pos_out = jnp.zeros((tm, TOP_K), F32)
    for k in range(TOP_K):
        pk = jnp.sum(jnp.where(lane == idxs[k], ranks, 0.0), axis=-1, keepdims=True)
        idx_out = jnp.where(slot == k, idxs[k], idx_out)
        w_out = jnp.where(slot == k, ex[k] / den, w_out)
        pos_out = jnp.where(slot == k, pk, pos_out)
    idx_ref[...] = idx_out
    w_ref[...] = w_out
    pos_ref[...] = pos_out.astype(jnp.int32)
    carry_ref[...] += jnp.sum(sel, axis=0, keepdims=True)
    cnt_ref[...] = carry_ref[...]


def _route(x, g, sc, sh, w_router, b_router):
    s, d = x.shape
    n_e = w_router.shape[1]
    half = d // 2
    tm = _tile(s, 512, 16)
    row = lambda i: (0, 0)
    tok = lambda i: (i, 0)
    return pl.pallas_call(
        _route_kernel,
        grid=(s // tm,),
        in_specs=[pl.BlockSpec((tm, d), tok),
                  pl.BlockSpec((1, d), row), pl.BlockSpec((1, d), row), pl.BlockSpec((1, d), row),
                  pl.BlockSpec((d, n_e), row), pl.BlockSpec((1, n_e), row)],
        out_specs=[pl.BlockSpec((tm, half), tok),
                   pl.BlockSpec((tm, TOP_K), tok), pl.BlockSpec((tm, TOP_K), tok),
                   pl.BlockSpec((tm, TOP_K), tok), pl.BlockSpec((1, n_e), row)],
        out_shape=[jax.ShapeDtypeStruct((s, half), jnp.uint32),
                   jax.ShapeDtypeStruct((s, TOP_K), jnp.int32),
                   jax.ShapeDtypeStruct((s, TOP_K), F32),
                   jax.ShapeDtypeStruct((s, TOP_K), jnp.int32),
                   jax.ShapeDtypeStruct((1, n_e), F32)],
        scratch_shapes=[pltpu.VMEM((tm, n_e), F32), pltpu.VMEM((1, n_e), F32)],
        compiler_params=_params(("arbitrary",), 40),
        name="moe_route",
    )(x, g, sc, sh, w_router, b_router.reshape(1, n_e))


def _dispatch_kernel(dest_hbm, hp_ref, hs_hbm, dest_smem, idx_sem, row_sem):
    n = dest_smem.shape[0]
    base = pl.multiple_of(pl.program_id(0) * n, n)
    idx_copy = pltpu.make_async_copy(dest_hbm.at[pl.ds(base, n)], dest_smem, idx_sem)
    idx_copy.start()
    idx_copy.wait()

    def row_copy(t, k):
        dst = dest_smem[t * TOP_K + k]
        return pltpu.make_async_copy(hp_ref.at[pl.ds(t, 1), :], hs_hbm.at[pl.ds(dst, 1), :], row_sem)

    def issue(t, carry):
        for k in range(TOP_K):
            row_copy(t, k).start()
        return carry

    def drain(t, carry):
        for k in range(TOP_K):
            row_copy(t, k).wait()
        return carry

    lax.fori_loop(0, n // TOP_K, issue, 0, unroll=2)
    lax.fori_loop(0, n // TOP_K, drain, 0, unroll=2)


def _dispatch(hp, dest):
    m = dest.shape[0]
    s, half = hp.shape
    tm = _tile(s, 256, 256)
    return pl.pallas_call(
        _dispatch_kernel,
        grid=(s // tm,),
        in_specs=[pl.BlockSpec(memory_space=pl.ANY),
                  pl.BlockSpec((tm, half), lambda i: (i, 0))],
        out_specs=pl.BlockSpec(memory_space=pl.ANY),
        out_shape=jax.ShapeDtypeStruct((m, half), hp.dtype),
        scratch_shapes=[pltpu.SMEM((tm * TOP_K,), jnp.int32),
                        pltpu.SemaphoreType.DMA, pltpu.SemaphoreType.DMA],
        compiler_params=_params(("arbitrary",), 16),
        name="moe_dispatch",
    )(dest, hp)


def _experts_kernel(vt_ref, ve_ref, lo_ref, hi_ref, hs_ref, wgu_ref, bgu_ref, wdn_ref, o_ref):
    v = pl.program_id(0)
    lo = lo_ref[v]
    hi = hi_ref[v]
    f, d = wdn_ref.shape
    tr, half = hs_ref.shape

    @pl.when(hi > lo)
    def _():
        hu = hs_ref[...]
        h_lo = _unpack_lo(hu).astype(BF16)
        h_hi = _unpack_hi(hu).astype(BF16)
        gu = _dot(h_lo, wgu_ref[:half, :]) + _dot(h_hi, wgu_ref[half:, :]) + bgu_ref[...]
        gt = jnp.minimum(gu[:, :f], SWIGLU_LIMIT)
        ln = jnp.clip(gu[:, f:], -SWIGLU_LIMIT, SWIGLU_LIMIT)
        act = (ln + 1.0) * gt * jax.nn.sigmoid(SWIGLU_ALPHA * gt)
        packed = _pack_rows(_dot(act.astype(BF16), wdn_ref[...]))
        row = lax.broadcasted_iota(jnp.int32, (tr, 1), 0)
        mine = (row >= lo) & (row < hi)

        @pl.when(lo == 0)
        def _():
            o_ref[...] = jnp.where(mine, packed, jnp.uint32(0))

        @pl.when(lo > 0)
        def _():
            o_ref[...] = jnp.where(mine, packed, o_ref[...])


EXPERT_TILE = 256


def _experts(hs, visits, w_gu, b_gu, w_dn, tr):
    vt, ve, vlo, vhi = visits
    n_e, d, f2 = w_gu.shape
    f = f2 // 2
    half = hs.shape[1]
    grid_spec = pltpu.PrefetchScalarGridSpec(
        num_scalar_prefetch=4,
        grid=(vt.shape[0],),
        in_specs=[pl.BlockSpec((tr, half), lambda v, vt, ve, lo, hi: (vt[v], 0)),
                  pl.BlockSpec((None, d, f2), lambda v, vt, ve, lo, hi: (ve[v], 0, 0)),
                  pl.BlockSpec((None, 1, f2), lambda v, vt, ve, lo, hi: (ve[v], 0, 0)),
                  pl.BlockSpec((None, f, d), lambda v, vt, ve, lo, hi: (ve[v], 0, 0))],
        out_specs=pl.BlockSpec((tr, half), lambda v, vt, ve, lo, hi: (vt[v], 0)),
    )
    return pl.pallas_call(
        _experts_kernel, grid_spec=grid_spec,
        out_shape=jax.ShapeDtypeStruct(hs.shape, jnp.uint32),
        compiler_params=_params(("arbitrary",), 48),
        name="moe_experts",
    )(vt, ve, vlo, vhi, hs, w_gu, b_gu.reshape(n_e, 1, f2), w_dn)


def _combine_kernel(dest_hbm, ys_hbm, w_ref, idx_ref, bd_ref, x_ref, gate_ref,
                    o_ref, bias_ref, buf_ref, dest_smem, idx_sem, row_sem):
    i = pl.program_id(0)
    tm, d = x_ref.shape
    half = d // 2
    n_e = bd_ref.shape[0]
    n = tm * TOP_K
    slot = i % 2

    def fetch_indices(tile, sl):
        base = pl.multiple_of(tile * n, n)
        copy = pltpu.make_async_copy(dest_hbm.at[pl.ds(base, n)], dest_smem.at[sl], idx_sem)
        copy.start()
        copy.wait()

    def row_copy(sl, t, k):
        src = dest_smem[sl, t * TOP_K + k]
        return pltpu.make_async_copy(ys_hbm.at[pl.ds(src, 1), :],
                                     buf_ref.at[sl, k, pl.ds(t, 1), :], row_sem.at[sl])

    def request_rows(sl):
        def issue(t, carry):
            for k in range(TOP_K):
                row_copy(sl, t, k).start()
            return carry
        lax.fori_loop(0, tm, issue, 0, unroll=2)

    @pl.when(i == 0)
    def _():
        fetch_indices(0, 0)
        request_rows(0)

    @pl.when(i + 1 < pl.num_programs(0))
    def _():
        fetch_indices(i + 1, 1 - slot)
        request_rows(1 - slot)

    w = w_ref[...]
    idx = idx_ref[...]
    lane = lax.broadcasted_iota(jnp.int32, (tm, n_e), 1)
    comb = jnp.zeros((tm, n_e), F32)
    for k in range(TOP_K):
        comb = comb + jnp.where(lane == idx[:, k:k + 1], w[:, k:k + 1], 0.0)
    bias_ref[...] = _dot_f32ish(comb, bd_ref[...])

    def drain(t, carry):
        for k in range(TOP_K):
            row_copy(slot, t, k).wait()
        return carry
    lax.fori_loop(0, tm, drain, 0, unroll=2)

    wb = [jnp.broadcast_to(w[:, k:k + 1], (tm, LANES)) for k in range(TOP_K)]
    for c in range(half // LANES):
        lo = jnp.zeros((tm, LANES), F32)
        hi = jnp.zeros((tm, LANES), F32)
        for k in range(TOP_K):
            u = buf_ref[slot, k, :, c * LANES:(c + 1) * LANES]
            lo = lo + wb[k] * _unpack_lo(u)
            hi = hi + wb[k] * _unpack_hi(u)
        cl = slice(c * LANES, (c + 1) * LANES)
        ch = slice(half + c * LANES, half + (c + 1) * LANES)
        o_ref[:, cl] = x_ref[:, cl] + gate_ref[:, cl] * (lo + bias_ref[:, cl])
        o_ref[:, ch] = x_ref[:, ch] + gate_ref[:, ch] * (hi + bias_ref[:, ch])


def _combine(ys, dest, w4, idx4, b_dn, x, gate):
    s, d = x.shape
    n_e = b_dn.shape[0]
    tm = _tile(s, 256, 256)
    tok = lambda i: (i, 0)
    row = lambda i: (0, 0)
    any_spec = pl.BlockSpec(memory_space=pl.ANY)
    return pl.pallas_call(
        _combine_kernel,
        grid=(s // tm,),
        in_specs=[any_spec, any_spec,
                  pl.BlockSpec((tm, TOP_K), tok), pl.BlockSpec((tm, TOP_K), tok),
                  pl.BlockSpec((n_e, d), row), pl.BlockSpec((tm, d), tok),
                  pl.BlockSpec((1, d), row)],
        out_specs=pl.BlockSpec((tm, d), tok),
        out_shape=jax.ShapeDtypeStruct((s, d), F32),
        scratch_shapes=[pltpu.VMEM((tm, d), F32),
                        pltpu.VMEM((2, TOP_K, tm, d // 2), jnp.uint32),
                        pltpu.SMEM((2, tm * TOP_K), jnp.int32),
                        pltpu.SemaphoreType.DMA, pltpu.SemaphoreType.DMA((2,))],
        compiler_params=_params(("arbitrary",), 52),
        name="moe_combine",
    )(dest, ys, w4, idx4, b_dn, x, gate)


def _expert_visits(off, m, tr, n_e):
    nt = m // tr
    pts = jnp.sort(jnp.concatenate([jnp.arange(nt, dtype=jnp.int32) * tr, off[:n_e]]))
    nxt = jnp.concatenate([pts[1:], jnp.full((1,), m, jnp.int32)])
    vt = jnp.minimum(pts // tr, nt - 1)
    ve = jnp.minimum(jnp.sum(off[None, 1:] <= pts[:, None], axis=1), n_e - 1).astype(jnp.int32)
    return vt, ve, pts - vt * tr, nxt - vt * tr


def _moe(x, g, sc, sh, gate, w_router, b_router, w_gu, b_gu, w_dn, b_dn):
    s, d = x.shape
    n_e = w_router.shape[1]
    m = TOP_K * s
    tr = _tile(m, EXPERT_TILE, 16)
    hp, idx4, w4, pos4, counts = _route(x, g, sc, sh, w_router, b_router)
    counts = counts.reshape(n_e).astype(jnp.int32)
    off = jnp.concatenate([jnp.zeros((1,), jnp.int32), jnp.cumsum(counts)])
    dest = (off[idx4] + pos4).reshape(m)
    hs = _dispatch(hp, dest)
    ys = _experts(hs, _expert_visits(off, m, tr, n_e), w_gu, b_gu, w_dn, tr)
    return _combine(ys, dest, w4, idx4, b_dn, x, gate)


def _prep_gu_kernel(w_ref, o_ref):
    n = w_ref.shape[1]
    src = lax.broadcasted_iota(jnp.int32, (n, n), 0)
    dst = lax.broadcasted_iota(jnp.int32, (n, n), 1)
    want = jnp.where(dst < n // 2, 2 * dst, 2 * (dst - n // 2) + 1)
    perm = jnp.where(src == want, 1.0, 0.0).astype(BF16)
    o_ref[...] = _dot(w_ref[...].astype(BF16), perm).astype(BF16)


def _prep_gu(w_gu):
    shape = w_gu.shape
    n = shape[-1]
    rows = w_gu.size // n
    tm = _tile(rows, 2048, 16)
    out = pl.pallas_call(
        _prep_gu_kernel, grid=(rows // tm,),
        in_specs=[pl.BlockSpec((tm, n), lambda i: (i, 0))],
        out_specs=pl.BlockSpec((tm, n), lambda i: (i, 0)),
        out_shape=jax.ShapeDtypeStruct((rows, n), BF16),
        compiler_params=_params(("arbitrary",), 32),
        name="prep_expert_up",
    )(w_gu.reshape(rows, n))
    return out.reshape(shape)


def _fcum_kernel(fl_ref, o_ref, carry_ref):
    @pl.when(pl.program_id(0) == 0)
    def _():
        carry_ref[...] = jnp.zeros(carry_ref.shape, F32)

    x = fl_ref[...]
    log_f = jnp.minimum(x, 0.0) - jnp.log1p(jnp.exp(-jnp.abs(x)))
    tm = x.shape[0]
    r = lax.broadcasted_iota(jnp.int32, (tm, tm), 0)
    c = lax.broadcasted_iota(jnp.int32, (tm, tm), 1)
    tri = jnp.where(c <= r, 1.0, 0.0).astype(BF16)
    hi, mid, lo = _split3(log_f)
    out = (_dot(tri, hi) + _dot(tri, mid) + _dot(tri, lo)) + carry_ref[...]
    o_ref[...] = out
    carry_ref[...] = out[tm - 1:tm, :]


def _forget_cumsum(fl):
    s, h = fl.shape
    tm = _tile(s, 256, 8)
    return pl.pallas_call(
        _fcum_kernel, grid=(s // tm,),
        in_specs=[pl.BlockSpec((tm, h), lambda i: (i, 0))],
        out_specs=pl.BlockSpec((tm, h), lambda i: (i, 0)),
        out_shape=jax.ShapeDtypeStruct((s, h), F32),
        scratch_shapes=[pltpu.VMEM((1, h), F32)],
        compiler_params=_params(("arbitrary",), 16),
        name="forget_cumsum",
    )(fl)


def _aug_kernel(src_ref, f_ref, g_ref, shift_ref, o_ref, *, mult, is_query):
    tm = src_ref.shape[0]
    n_heads = f_ref.shape[1]
    gm = g_ref[...] * mult
    lane = lax.broadcasted_iota(jnp.int32, (tm, HEAD_DIM), 1)
    for h in range(n_heads):
        t = src_ref[:, h * HEAD_DIM:(h + 1) * HEAD_DIM].astype(F32)
        ms = jnp.mean(t * t, axis=-1, keepdims=True)
        o_ref[:, h * AUG_DIM:h * AUG_DIM + HEAD_DIM] = (t * lax.rsqrt(ms + EPS) * gm).astype(BF16)
        f = jnp.broadcast_to(f_ref[:, h:h + 1] * LOG2E - shift_ref[...], (tm, HEAD_DIM))
        hi, mid, lo = [p.astype(F32) for p in _split3(f)]
        one = jnp.ones((tm, HEAD_DIM), F32)
        if is_query:
            pieces = (hi, mid, lo, one, one, one)
        else:
            pieces = (one, one, one, -hi, -mid, -lo)
        extra = jnp.zeros((tm, HEAD_DIM), F32)
        for p, piece in enumerate(pieces):
            extra = jnp.where(lane == p, piece, extra)
        o_ref[:, h * AUG_DIM + HEAD_DIM:(h + 1) * AUG_DIM] = extra.astype(BF16)


def _augment(src, f_cum, g, shift, *, mult, is_query):
    s = src.shape[0]
    n_heads = f_cum.shape[1]
    d = n_heads * HEAD_DIM
    tm = _tile(s, 256, 16)
    return pl.pallas_call(
        functools.partial(_aug_kernel, mult=mult, is_query=is_query),
        grid=(s // tm,),
        in_specs=[pl.BlockSpec((tm, d), lambda i: (i, 0)),
                  pl.BlockSpec((tm, n_heads), lambda i: (i, 0)),
                  pl.BlockSpec((1, HEAD_DIM), lambda i: (0, 0)),
                  pl.BlockSpec((1, 1), lambda i: (0, 0))],
        out_specs=pl.BlockSpec((tm, n_heads * AUG_DIM), lambda i: (i, 0)),
        out_shape=jax.ShapeDtypeStruct((s, n_heads * AUG_DIM), BF16),
        compiler_params=_params(("arbitrary",), 32),
        name="augment_q" if is_query else "augment_k",
    )(src, f_cum, g.reshape(1, HEAD_DIM), jnp.asarray(shift, F32).reshape(1, 1))


def _aug_v_kernel(v_ref, o_ref):
    tm = v_ref.shape[0]
    one = jnp.ones((tm, HEAD_DIM), BF16)
    for h in range(v_ref.shape[1] // HEAD_DIM):
        o_ref[:, h * AUG_DIM:h * AUG_DIM + HEAD_DIM] = v_ref[:, h * HEAD_DIM:(h + 1) * HEAD_DIM]
        o_ref[:, h * AUG_DIM + HEAD_DIM:(h + 1) * AUG_DIM] = one


def _augment_v(kv, d):
    s = kv.shape[0]
    n_heads = d // HEAD_DIM
    tm = _tile(s, 256, 16)
    return pl.pallas_call(
        _aug_v_kernel, grid=(s // tm,),
        in_specs=[pl.BlockSpec((tm, d), lambda i: (i, 1))],
        out_specs=pl.BlockSpec((tm, n_heads * AUG_DIM), lambda i: (i, 0)),
        out_shape=jax.ShapeDtypeStruct((s, n_heads * AUG_DIM), BF16),
        compiler_params=_params(("arbitrary",), 32),
        name="augment_v",
    )(kv)


def _attn_kernel(start_ref, q_ref, k_ref, v_ref, o_ref, m_ref, l_ref, acc_ref, *, tk):
    h = pl.program_id(0)
    qi = pl.program_id(1)
    n_sub = q_ref.shape[0] // tk
    m_ref[...] = jnp.full(m_ref.shape, MASK_VALUE, F32)
    l_ref[...] = jnp.zeros(l_ref.shape, F32)
    acc_ref[...] = jnp.zeros(acc_ref.shape, F32)

    def sub_step(a, kb, on_diagonal):
        qrows = slice(a * tk, (a + 1) * tk)
        krows = pl.ds(pl.multiple_of(kb * tk, tk), tk)
        s = lax.dot_general(q_ref[qrows, :], k_ref[krows, :], (((1,), (1,)), ((), ())),
                            preferred_element_type=F32)
        if on_diagonal:
            r = lax.broadcasted_iota(jnp.int32, s.shape, 0)
            c = lax.broadcasted_iota(jnp.int32, s.shape, 1)
            s = jnp.where(c <= r, s, MASK_VALUE)
        m_prev = m_ref[qrows, :]
        m_new = jnp.maximum(m_prev, jnp.max(s, axis=-1, keepdims=True))
        alpha = jnp.exp2(m_prev - m_new)
        p = jnp.exp2(s - m_new)
        l_ref[qrows, :] = alpha * l_ref[qrows, :] + jnp.sum(p, axis=-1, keepdims=True)
        acc_ref[qrows, :] = alpha * acc_ref[qrows, :] + _dot(p.astype(BF16), v_ref[krows, :])
        m_ref[qrows, :] = m_new

    def body(kb, carry):
        for a in range(n_sub):
            sub_step(a, kb, False)
        return carry

    first_diag = qi * n_sub
    lax.fori_loop(start_ref[h, qi], first_diag, body, 0)
    for a in range(n_sub):
        for b in range(a):
            sub_step(a, first_diag + b, False)
        sub_step(a, first_diag + a, True)
    o_ref[...] = (acc_ref[...] / l_ref[...]).astype(o_ref.dtype)


def _attention(q_aug, k_aug, kv, start, n_heads, tq, tk):
    s = q_aug.shape[0]
    nq = s // tq
    grid_spec = pltpu.PrefetchScalarGridSpec(
        num_scalar_prefetch=1,
        grid=(n_heads, nq),
        in_specs=[pl.BlockSpec((tq, AUG_DIM), lambda h, qi, st: (qi, h)),
                  pl.BlockSpec((s, AUG_DIM), lambda h, qi, st: (0, h)),
                  pl.BlockSpec((s, HEAD_DIM), lambda h, qi, st: (0, n_heads + h))],
        out_specs=pl.BlockSpec((tq, HEAD_DIM), lambda h, qi, st: (qi, h)),
        scratch_shapes=[pltpu.VMEM((tq, 1), F32), pltpu.VMEM((tq, 1), F32),
                        pltpu.VMEM((tq, HEAD_DIM), F32)],
    )
    return pl.pallas_call(
        functools.partial(_attn_kernel, tk=tk), grid_spec=grid_spec,
        out_shape=jax.ShapeDtypeStruct((s, n_heads * HEAD_DIM), BF16),
        compiler_params=_params(("arbitrary", "arbitrary"), 48),
        name="fox_attention",
    )(start, q_aug, k_aug, kv)


def _attn_bounded_kernel(start_ref, q_ref, k_ref, v_ref, o_ref, acc_ref, *, tk):
    h = pl.program_id(0)
    qi = pl.program_id(1)
    n_sub = q_ref.shape[0] // tk
    acc_ref[...] = jnp.zeros(acc_ref.shape, F32)

    def sub_step(a, kb, on_diagonal):
        qrows = slice(a * tk, (a + 1) * tk)
        krows = pl.ds(pl.multiple_of(kb * tk, tk), tk)
        s = lax.dot_general(q_ref[qrows, :], k_ref[krows, :], (((1,), (1,)), ((), ())),
                            preferred_element_type=F32)
        p = jnp.exp2(s)
        if on_diagonal:
            r = lax.broadcasted_iota(jnp.int32, s.shape, 0)
            c = lax.broadcasted_iota(jnp.int32, s.shape, 1)
            p = jnp.where(c <= r, p, 0.0)
        acc_ref[qrows, :] += _dot(p.astype(BF16), v_ref[krows, :])

    def body(kb, carry):
        for a in range(n_sub):
            sub_step(a, kb, False)
        return carry

    first_diag = qi * n_sub
    lax.fori_loop(start_ref[h, qi], first_diag, body, 0)
    for a in range(n_sub):
        for b in range(a):
            sub_step(a, first_diag + b, False)
        sub_step(a, first_diag + a, True)
    acc = acc_ref[...]
    o_ref[...] = (acc[:, :HEAD_DIM] / acc[:, HEAD_DIM:]).astype(o_ref.dtype)


def _attention_bounded(q_aug, k_aug, v_aug, start, n_heads, tq, tk):
    s = q_aug.shape[0]
    grid_spec = pltpu.PrefetchScalarGridSpec(
        num_scalar_prefetch=1,
        grid=(n_heads, s // tq),
        in_specs=[pl.BlockSpec((tq, AUG_DIM), lambda h, qi, st: (qi, h)),
                  pl.BlockSpec((s, AUG_DIM), lambda h, qi, st: (0, h)),
                  pl.BlockSpec((s, AUG_DIM), lambda h, qi, st: (0, h))],
        out_specs=pl.BlockSpec((tq, HEAD_DIM), lambda h, qi, st: (qi, h)),
        scratch_shapes=[pltpu.VMEM((tq, AUG_DIM), F32)],
    )
    return pl.pallas_call(
        functools.partial(_attn_bounded_kernel, tk=tk), grid_spec=grid_spec,
        out_shape=jax.ShapeDtypeStruct((s, n_heads * HEAD_DIM), BF16),
        compiler_params=_params(("arbitrary", "arbitrary"), 52),
        name="fox_attention_bounded",
    )(start, q_aug, k_aug, v_aug)


MAX_BOUNDED_LOGIT_RANGE = 80.0


def _logit_bound(q_norm_g, k_norm_g):
    return HEAD_DIM ** 0.5 * jnp.max(jnp.abs(q_norm_g * k_norm_g))


def _attention_start_blocks(f_cum, q_norm_g, k_norm_g, tq, tk):
    s, n_heads = f_cum.shape
    nq = s // tq
    nk = s // tk
    thresh = F32_EXP_UNDERFLOW + 2.0 * _logit_bound(q_norm_g, k_norm_g) + 1.0
    f_max_q = jnp.max(f_cum.reshape(nq, tq, n_heads), axis=1).T
    f_min_k = jnp.min(f_cum.reshape(nk, tk, n_heads), axis=1).T
    needed = (f_max_q[:, :, None] - f_min_k[:, None, :]) >= -thresh
    first_diag = jnp.arange(nq) * (tq // tk)
    needed = needed | (jnp.arange(nk)[None, None, :] >= first_diag[None, :, None])
    return jnp.argmax(needed, axis=-1).astype(jnp.int32)


def _gateout_kernel(o_ref, gp_ref, wo_ref, x_ref, gate_ref, out_ref, a_ref):
    @pl.when(pl.program_id(1) == 0)
    def _():
        a = o_ref[...].astype(F32) * jax.nn.sigmoid(gp_ref[...].astype(F32))
        a_ref[...] = a.astype(a_ref.dtype)

    out_ref[...] = x_ref[...] + gate_ref[...] * _dot(a_ref[...], wo_ref[...])


def _gate_out(o, qg, w_o, x, gate):
    s, d = x.shape
    tm = _tile(s, 512, 16)
    tn = _tile(d, WIDE_TN, 128)
    return pl.pallas_call(
        _gateout_kernel,
        grid=(s // tm, d // tn),
        in_specs=[pl.BlockSpec((tm, d), lambda i, j: (i, 0)),
                  pl.BlockSpec((tm, d), lambda i, j: (i, 1)),
                  pl.BlockSpec((d, tn), lambda i, j: (0, j)),
                  pl.BlockSpec((tm, tn), lambda i, j: (i, j)),
                  pl.BlockSpec((1, tn), lambda i, j: (0, j))],
        out_specs=pl.BlockSpec((tm, tn), lambda i, j: (i, j)),
        out_shape=jax.ShapeDtypeStruct((s, d), F32),
        scratch_shapes=[pltpu.VMEM((tm, d), BF16)],
        compiler_params=_params(("arbitrary", "arbitrary"), 48),
        name="gate_out",
    )(o, qg, w_o, x, gate)


ATTN_KV_BLOCK = 512
ATTN_Q_SUBTILES = 2


def kernel(x, c, w_ada, b_ada, ada_table, norm1_g, norm2_g, a_w_in, a_w_grp, a_b_grp, a_scale,
           a_w_out, kv_norm_g, w_kvf, b_f, k_norm_g, b_w_qg, q_norm_g, b_w_o, moe_w_router,
           moe_b_router, moe_w_gu, moe_b_gu, moe_w_dn, moe_b_dn):
    b, s, d = x.shape
    assert b == 1, "single-sequence trunk"
    depth = ada_table.shape[0]
    n_mod = ada_table.shape[1]
    n_a = a_w_in.shape[0]
    n_heads = d // HEAD_DIM
    tk = _tile(s, ATTN_KV_BLOCK, 16)
    tq = tk * ATTN_Q_SUBTILES if s % (tk * ATTN_Q_SUBTILES) == 0 else tk
    xs = x.reshape(s, d)

    mod = _ada(c, w_ada, b_ada).reshape(n_mod, d)
    zeros_row = jnp.zeros((1, d), F32)
    w_gu_all = _prep_gu(moe_w_gu)
    b_gu_all = jnp.concatenate([moe_b_gu[..., 0::2], moe_b_gu[..., 1::2]], axis=-1)
    k_aug = v_aug = kv = f_cum = None

    for layer in range(depth):
        m = mod + ada_table[layer]
        shift1, scale1, gate1, shift2, scale2, gate2 = [m[i:i + 1] for i in range(n_mod)]
        g1 = norm1_g[layer].reshape(1, d)
        g2 = norm2_g[layer].reshape(1, d)
        if layer < n_a:
            p = _in_proj_pool(xs, g1, scale1, shift1, a_w_in[layer].astype(BF16))
            xs = _pool_out(p, a_w_grp[layer].astype(BF16), a_b_grp[layer], a_scale[layer],
                           a_w_out[layer].astype(BF16), xs, gate1)
        else:
            j = layer - n_a
            qg = _norm_matmul(xs, g1, scale1, shift1, b_w_qg[j].astype(BF16))
            bound2 = LOG2E * (_logit_bound(q_norm_g[j], k_norm_g) + 1.0)
            bounded = 2.0 * bound2 <= MAX_BOUNDED_LOGIT_RANGE
            q_aug = _augment(qg, f_cum, q_norm_g[j], jnp.where(bounded, bound2, 0.0),
                             mult=HEAD_DIM ** -0.5 * LOG2E, is_query=True)
            start = _attention_start_blocks(f_cum, q_norm_g[j], k_norm_g, tq, tk)
            o = lax.cond(
                bounded,
                lambda qa, ka, va, kv_, st: _attention_bounded(qa, ka, va, st, n_heads, tq, tk),
                lambda qa, ka, va, kv_, st: _attention(qa, ka, kv_, st, n_heads, tq, tk),
                q_aug, k_aug, v_aug, kv, start)
            xs = _gate_out(o, qg, b_w_o[j].astype(BF16), xs, gate1)

        xs = _moe(xs, g2, scale2, shift2, gate2, moe_w_router[layer], moe_b_router[layer],
                  w_gu_all[layer], b_gu_all[layer], moe_w_dn[layer].astype(BF16), moe_b_dn[layer])

        if layer == n_a - 1:
            kv, fl = _norm_matmul(xs, kv_norm_g.reshape(1, d), zeros_row, zeros_row,
                                  w_kvf[:, :2 * d].astype(BF16),
                                  side=(w_kvf[:, 2 * d:].astype(BF16), b_f.reshape(1, n_heads)))
            f_cum = _forget_cumsum(fl)
            k_aug = _augment(kv, f_cum, k_norm_g, 0.0, mult=1.0, is_query=False)
            v_aug = _augment_v(kv, d)

    return xs.reshape(b, s, d)
```

```python
import functools

import jax
import jax.numpy as jnp
from jax import lax
from jax.experimental import pallas as pl
from jax.experimental.pallas import tpu as pltpu

F32 = jnp.float32
BF16 = jnp.bfloat16

HEAD_DIM = 128
POOL_WINDOWS = (2, 4, 8, 16)
TOP_K = 4
SWIGLU_LIMIT = 7.0
SWIGLU_ALPHA = 1.702
EPS = 1e-6
LOG2E = 1.4426950408889634

AUG_DIM = 2 * HEAD_DIM
F32_EXP_UNDERFLOW = 88.0
MASK_VALUE = -1e30
MIB = 1 << 20


def _tile(n, pref, align):
    if n <= pref:
        return n
    t = (pref // align) * align
    while t >= align:
        if n % t == 0:
            return t
        t -= align
    raise ValueError(f"no tile for {n} (pref {pref}, align {align})")


def _params(semantics, vmem_mib):
    return pltpu.CompilerParams(dimension_semantics=semantics, vmem_limit_bytes=vmem_mib * MIB)


def _split3(a):
    hi = a.astype(BF16)
    r1 = a - hi.astype(F32)
    mid = r1.astype(BF16)
    lo = (r1 - mid.astype(F32)).astype(BF16)
    return hi, mid, lo


def _dot(a, b):
    return jnp.dot(a, b, preferred_element_type=F32)


def _dot_f32ish(a, b):
    ah = a.astype(BF16)
    al = (a - ah.astype(F32)).astype(BF16)
    bh = b.astype(BF16)
    bl = (b - bh.astype(F32)).astype(BF16)
    return _dot(ah, bh) + _dot(al, bh) + _dot(ah, bl)


def _modulated_norm(x, gs, sh):
    ms = jnp.mean(x * x, axis=-1, keepdims=True)
    return x * lax.rsqrt(ms + EPS) * gs + sh


NORM_ROWS = 64
WIDE_TN = 1024


def _norm_rows_into(x_ref, g_ref, sc_ref, sh_ref, h_ref):
    tm = x_ref.shape[0]
    rows = min(NORM_ROWS, tm)
    gs = g_ref[...] * (1.0 + sc_ref[...])
    sh = sh_ref[...]

    def body(r, carry):
        sl = pl.ds(pl.multiple_of(r * rows, rows), rows)
        h_ref[sl, :] = _modulated_norm(x_ref[sl, :], gs, sh).astype(h_ref.dtype)
        return carry

    lax.fori_loop(0, tm // rows, body, 0)


def _ada_kernel(c_ref, w_ref, b_ref, o_ref):
    c = c_ref[...]
    s = c * jax.nn.sigmoid(c)
    o_ref[...] = jnp.sum(w_ref[...] * s, axis=0, keepdims=True) + b_ref[...]


def _ada(c, w_ada, b_ada):
    d, n = w_ada.shape
    tn = _tile(n, 512, 128)
    return pl.pallas_call(
        _ada_kernel,
        grid=(n // tn,),
        in_specs=[pl.BlockSpec((d, 1), lambda j: (0, 0)),
                  pl.BlockSpec((d, tn), lambda j: (0, j)),
                  pl.BlockSpec((1, tn), lambda j: (0, j))],
        out_specs=pl.BlockSpec((1, tn), lambda j: (0, j)),
        out_shape=jax.ShapeDtypeStruct((1, n), F32),
        compiler_params=_params(("arbitrary",), 40),
        name="ada_proj",
    )(c.reshape(d, 1), w_ada, b_ada.reshape(1, n))


def _nmm_kernel(x_ref, g_ref, sc_ref, sh_ref, w_ref, o_ref, h_ref):
    @pl.when(pl.program_id(1) == 0)
    def _():
        _norm_rows_into(x_ref, g_ref, sc_ref, sh_ref, h_ref)

    o_ref[...] = _dot(h_ref[...], w_ref[...]).astype(o_ref.dtype)


def _nmm_side_kernel(x_ref, g_ref, sc_ref, sh_ref, w_ref, ws_ref, bs_ref, o_ref, os_ref, h_ref):
    @pl.when(pl.program_id(1) == 0)
    def _():
        _norm_rows_into(x_ref, g_ref, sc_ref, sh_ref, h_ref)
        os_ref[...] = _dot(h_ref[...], ws_ref[...]) + bs_ref[...]

    o_ref[...] = _dot(h_ref[...], w_ref[...]).astype(o_ref.dtype)


def _norm_matmul(x, g, sc, sh, w, side=None):
    s, d = x.shape
    n = w.shape[1]
    tm = _tile(s, 512, 16)
    tn = _tile(n, WIDE_TN, 128)
    row = lambda i, j: (0, 0)
    in_specs = [pl.BlockSpec((tm, d), lambda i, j: (i, 0)),
                pl.BlockSpec((1, d), row), pl.BlockSpec((1, d), row), pl.BlockSpec((1, d), row),
                pl.BlockSpec((d, tn), lambda i, j: (0, j))]
    out_spec = pl.BlockSpec((tm, tn), lambda i, j: (i, j))
    out_shape = jax.ShapeDtypeStruct((s, n), BF16)
    scratch = [pltpu.VMEM((tm, d), BF16)]
    cp = _params(("arbitrary", "arbitrary"), 48)
    if side is None:
        return pl.pallas_call(_nmm_kernel, grid=(s // tm, n // tn), in_specs=in_specs,
                              out_specs=out_spec, out_shape=out_shape, scratch_shapes=scratch,
                              compiler_params=cp, name="norm_matmul")(x, g, sc, sh, w)
    ws, bs = side
    ns = ws.shape[1]
    in_specs += [pl.BlockSpec((d, ns), row), pl.BlockSpec((1, ns), row)]
    return pl.pallas_call(
        _nmm_side_kernel, grid=(s // tm, n // tn), in_specs=in_specs,
        out_specs=[out_spec, pl.BlockSpec((tm, ns), lambda i, j: (i, 0))],
        out_shape=[out_shape, jax.ShapeDtypeStruct((s, ns), F32)],
        scratch_shapes=scratch, compiler_params=cp, name="norm_matmul_side",
    )(x, g, sc, sh, w, ws, bs)


POOL_HALO = 16


def _inpool_kernel(x_ref, g_ref, sc_ref, sh_ref, w_ref, o_ref, h_ref, u_ref, carry_ref, *,
                   cols_per_group):
    i = pl.program_id(0)
    j = pl.program_id(1)
    tm, tn = o_ref.shape

    @pl.when(j == 0)
    def _():
        _norm_rows_into(x_ref, g_ref, sc_ref, sh_ref, h_ref)

    u = _dot(h_ref[...], w_ref[...])

    @pl.when(i == 0)
    def _():
        u_ref[0:POOL_HALO, :] = jnp.zeros((POOL_HALO, tn), F32)

    @pl.when(i > 0)
    def _():
        u_ref[0:POOL_HALO, :] = carry_ref[j]

    u_ref[POOL_HALO:, :] = u
    carry_ref[j] = u[tm - POOL_HALO:, :]

    t1 = (i * tm + lax.broadcasted_iota(jnp.int32, (tm, 1), 0) + 1).astype(F32)
    group = (j * tn) // cols_per_group
    for gi, w in enumerate(POOL_WINDOWS):
        @pl.when(group == gi)
        def _(w=w):
            acc = u_ref[POOL_HALO:, :]
            for k in range(1, w):
                acc = acc + u_ref[POOL_HALO - k:POOL_HALO - k + tm, :]
            cnt = jnp.minimum(t1, float(w))
            o_ref[...] = (acc / cnt - u_ref[POOL_HALO:, :]).astype(o_ref.dtype)


def _in_proj_pool(x, g, sc, sh, w_in):
    s, d = x.shape
    n = w_in.shape[1]
    cg = n // len(POOL_WINDOWS)
    tm = _tile(s, 512, 16)
    tn = _tile(cg, WIDE_TN, 128)
    row = lambda i, j: (0, 0)
    return pl.pallas_call(
        functools.partial(_inpool_kernel, cols_per_group=cg),
        grid=(s // tm, n // tn),
        in_specs=[pl.BlockSpec((tm, d), lambda i, j: (i, 0)),
                  pl.BlockSpec((1, d), row), pl.BlockSpec((1, d), row), pl.BlockSpec((1, d), row),
                  pl.BlockSpec((d, tn), lambda i, j: (0, j))],
        out_specs=pl.BlockSpec((tm, tn), lambda i, j: (i, j)),
        out_shape=jax.ShapeDtypeStruct((s, n), BF16),
        scratch_shapes=[pltpu.VMEM((tm, d), BF16),
                        pltpu.VMEM((POOL_HALO + tm, tn), F32),
                        pltpu.VMEM((n // tn, POOL_HALO, tn), F32)],
        compiler_params=_params(("arbitrary", "arbitrary"), 48),
        name="in_proj_pool",
    )(x, g, sc, sh, w_in)


def _poolout_kernel(p_ref, wg_ref, bg_ref, sg_ref, wo_ref, x_ref, gate_ref, o_ref, y_ref):
    @pl.when(pl.program_id(1) == 0)
    def _():
        n_groups, cg, _ = wg_ref.shape
        for gi in range(n_groups):
            cols = slice(gi * cg, (gi + 1) * cg)
            y = _dot(p_ref[:, cols], wg_ref[gi])
            y_ref[:, cols] = ((y + bg_ref[gi]) * sg_ref[gi]).astype(y_ref.dtype)

    o_ref[...] = x_ref[...] + gate_ref[...] * _dot(y_ref[...], wo_ref[...])


def _pool_out(p, w_grp, b_grp, scale, w_out, x, gate):
    s, d = x.shape
    ng, cg, _ = w_grp.shape
    tm = _tile(s, 512, 16)
    tn = _tile(d, 512, 128)
    const3 = lambda i, j: (0, 0, 0)
    return pl.pallas_call(
        _poolout_kernel,
        grid=(s // tm, d // tn),
        in_specs=[pl.BlockSpec((tm, d), lambda i, j: (i, 0)),
                  pl.BlockSpec((ng, cg, cg), const3),
                  pl.BlockSpec((ng, 1, cg), const3),
                  pl.BlockSpec((ng, 1, cg), const3),
                  pl.BlockSpec((d, tn), lambda i, j: (0, j)),
                  pl.BlockSpec((tm, tn), lambda i, j: (i, j)),
                  pl.BlockSpec((1, tn), lambda i, j: (0, j))],
        out_specs=pl.BlockSpec((tm, tn), lambda i, j: (i, j)),
        out_shape=jax.ShapeDtypeStruct((s, d), F32),
        scratch_shapes=[pltpu.VMEM((tm, d), BF16)],
        compiler_params=_params(("arbitrary", "arbitrary"), 52),
        name="pool_out",
    )(p, w_grp, b_grp.reshape(ng, 1, cg), scale.reshape(ng, 1, cg), w_out, x, gate)


LANES = 128
HIGH_HALF = 0xFFFF0000


def _pack_rows(v):
    half = v.shape[1] // 2
    lo = lax.bitcast_convert_type(v[:, :half].astype(BF16).astype(F32), jnp.uint32)
    hi = lax.bitcast_convert_type(v[:, half:].astype(BF16).astype(F32), jnp.uint32)
    return (lo >> 16) | (hi & jnp.uint32(HIGH_HALF))


def _unpack_lo(u):
    return lax.bitcast_convert_type(u << 16, F32)


def _unpack_hi(u):
    return lax.bitcast_convert_type(u & jnp.uint32(HIGH_HALF), F32)


def _top_k(logits):
    n_e = logits.shape[-1]
    lane = lax.broadcasted_iota(jnp.int32, logits.shape, 1)
    work = logits
    vals, idxs = [], []
    for _ in range(TOP_K):
        m = jnp.max(work, axis=-1, keepdims=True)
        idx = jnp.min(jnp.where(work == m, lane, n_e), axis=-1, keepdims=True)
        vals.append(m)
        idxs.append(idx)
        work = jnp.where(lane == idx, -jnp.inf, work)
    return vals, idxs


def _route_kernel(x_ref, g_ref, sc_ref, sh_ref, wr_ref, br_ref,
                  hp_ref, idx_ref, w_ref, pos_ref, cnt_ref, lg_ref, carry_ref):
    tm, d = x_ref.shape
    n_e = wr_ref.shape[1]
    rows = min(NORM_ROWS, tm)

    @pl.when(pl.program_id(0) == 0)
    def _():
        carry_ref[...] = jnp.zeros(carry_ref.shape, F32)

    gs = g_ref[...] * (1.0 + sc_ref[...])
    sh = sh_ref[...]
    wr = wr_ref[...]
    wr_hi = wr.astype(BF16)
    wr_lo = (wr - wr_hi.astype(F32)).astype(BF16)
    for r in range(tm // rows):
        sl = slice(r * rows, (r + 1) * rows)
        h = _modulated_norm(x_ref[sl, :], gs, sh)
        h_hi = h.astype(BF16)
        h_lo = (h - h_hi.astype(F32)).astype(BF16)
        lg_ref[sl, :] = _dot(h_hi, wr_hi) + _dot(h_lo, wr_hi) + _dot(h_hi, wr_lo)
        hp_ref[sl, :] = _pack_rows(h)

    vals, idxs = _top_k(lg_ref[...] + br_ref[...])
    ex = [jnp.exp(v - vals[0]) for v in vals]
    den = ex[0]
    for e in ex[1:]:
        den = den + e

    lane = lax.broadcasted_iota(jnp.int32, (tm, n_e), 1)
    sel = jnp.zeros((tm, n_e), F32)
    for idx in idxs:
        sel = jnp.where(lane == idx, 1.0, sel)
    r_i = lax.broadcasted_iota(jnp.int32, (tm, tm), 0)
    c_i = lax.broadcasted_iota(jnp.int32, (tm, tm), 1)
    below = jnp.where(c_i < r_i, 1.0, 0.0).astype(BF16)
    ranks = _dot(below, sel.astype(BF16)) + carry_ref[...]

    slot = lax.broadcasted_iota(jnp.int32, (tm, TOP_K), 1)
    idx_out = jnp.zeros((tm, TOP_K), jnp.int32)
    w_out = jnp.zeros((tm, TOP_K), F32)
    pos_out = jnp.zeros((tm, TOP_K), F32)
    for k in range(TOP_K):
        pk = jnp.sum(jnp.where(lane == idxs[k], ranks, 0.0), axis=-1, keepdims=True)
        idx_out = jnp.where(slot == k, idxs[k], idx_out)
        w_out = jnp.where(slot == k, ex[k] / den, w_out)
        pos_out = jnp.where(slot == k, pk, pos_out)
    idx_ref[...] = idx_out
    w_ref[...] = w_out
    pos_ref[...] = pos_out.astype(jnp.int32)
    carry_ref[...] += jnp.sum(sel, axis=0, keepdims=True)
    cnt_ref[...] = carry_ref[...]


def _route(x, g, sc, sh, w_router, b_router):
    s, d = x.shape
    n_e = w_router.shape[1]
    half = d // 2
    tm = _tile(s, 512, 16)
    row = lambda i: (0, 0)
    tok = lambda i: (i, 0)
    return pl.pallas_call(
        _route_kernel,
        grid=(s // tm,),
        in_specs=[pl.BlockSpec((tm, d), tok),
                  pl.BlockSpec((1, d), row), pl.BlockSpec((1, d), row), pl.BlockSpec((1, d), row),
                  pl.BlockSpec((d, n_e), row), pl.BlockSpec((1, n_e), row)],
        out_specs=[pl.BlockSpec((tm, half), tok),
                   pl.BlockSpec((tm, TOP_K), tok), pl.BlockSpec((tm, TOP_K), tok),
                   pl.BlockSpec((tm, TOP_K), tok), pl.BlockSpec((1, n_e), row)],
        out_shape=[jax.ShapeDtypeStruct((s, half), jnp.uint32),
                   jax.ShapeDtypeStruct((s, TOP_K), jnp.int32),
                   jax.ShapeDtypeStruct((s, TOP_K), F32),
                   jax.ShapeDtypeStruct((s, TOP_K), jnp.int32),
                   jax.ShapeDtypeStruct((1, n_e), F32)],
        scratch_shapes=[pltpu.VMEM((tm, n_e), F32), pltpu.VMEM((1, n_e), F32)],
        compiler_params=_params(("arbitrary",), 40),
        name="moe_route",
    )(x, g, sc, sh, w_router, b_router.reshape(1, n_e))


def _dispatch_kernel(dest_ref, hp_ref, hs_hbm, row_sem):
    tm = hp_ref.shape[0]
    base = pl.program_id(0) * tm

    def row_copy(t, k):
        dst = dest_ref[(base + t) * TOP_K + k]
        return pltpu.make_async_copy(hp_ref.at[pl.ds(t, 1), :], hs_hbm.at[pl.ds(dst, 1), :], row_sem)

    def issue(t, carry):
        for k in range(TOP_K):
            row_copy(t, k).start()
        return carry

    def drain(t, carry):
        for k in range(TOP_K):
            row_copy(t, k).wait()
        return carry

    lax.fori_loop(0, tm, issue, 0, unroll=2)
    lax.fori_loop(0, tm, drain, 0, unroll=2)


def _dispatch(hp, dest):
    m = dest.shape[0]
    s, half = hp.shape
    tm = _tile(s, 256, 16)
    grid_spec = pltpu.PrefetchScalarGridSpec(
        num_scalar_prefetch=1,
        grid=(s // tm,),
        in_specs=[pl.BlockSpec((tm, half), lambda i, dest: (i, 0))],
        out_specs=pl.BlockSpec(memory_space=pl.ANY),
        scratch_shapes=[pltpu.SemaphoreType.DMA],
    )
    return pl.pallas_call(
        _dispatch_kernel, grid_spec=grid_spec,
        out_shape=jax.ShapeDtypeStruct((m, half), hp.dtype),
        compiler_params=_params(("arbitrary",), 16),
        name="moe_dispatch",
    )(dest, hp)


def _experts_kernel(vt_ref, ve_ref, lo_ref, hi_ref, hs_ref, wgu_ref, bgu_ref, wdn_ref, bdn_ref,
                    o_ref):
    v = pl.program_id(0)
    lo = lo_ref[v]
    hi = hi_ref[v]
    f, d = wdn_ref.shape
    tr, half = hs_ref.shape

    @pl.when(hi > lo)
    def _():
        hu = hs_ref[...]
        h_lo = _unpack_lo(hu).astype(BF16)
        h_hi = _unpack_hi(hu).astype(BF16)
        gu = _dot(h_lo, wgu_ref[:half, :]) + _dot(h_hi, wgu_ref[half:, :]) + bgu_ref[...]
        gt = jnp.minimum(gu[:, :f], SWIGLU_LIMIT)
        ln = jnp.clip(gu[:, f:], -SWIGLU_LIMIT, SWIGLU_LIMIT)
        act = (ln + 1.0) * gt * jax.nn.sigmoid(SWIGLU_ALPHA * gt)
        packed = _pack_rows(_dot(act.astype(BF16), wdn_ref[...]) + bdn_ref[...])
        row = lax.broadcasted_iota(jnp.int32, (tr, 1), 0)
        mine = (row >= lo) & (row < hi)

        @pl.when(lo == 0)
        def _():
            o_ref[...] = jnp.where(mine, packed, jnp.uint32(0))

        @pl.when(lo > 0)
        def _():
            o_ref[...] = jnp.where(mine, packed, o_ref[...])


EXPERT_TILE = 256


def _experts(hs, visits, w_gu, b_gu, w_dn, b_dn, tr):
    vt, ve, vlo, vhi = visits
    n_e, d, f2 = w_gu.shape
    f = f2 // 2
    half = hs.shape[1]
    grid_spec = pltpu.PrefetchScalarGridSpec(
        num_scalar_prefetch=4,
        grid=(vt.shape[0],),
        in_specs=[pl.BlockSpec((tr, half), lambda v, vt, ve, lo, hi: (vt[v], 0)),
                  pl.BlockSpec((None, d, f2), lambda v, vt, ve, lo, hi: (ve[v], 0, 0)),
                  pl.BlockSpec((None, 1, f2), lambda v, vt, ve, lo, hi: (ve[v], 0, 0)),
                  pl.BlockSpec((None, f, d), lambda v, vt, ve, lo, hi: (ve[v], 0, 0)),
                  pl.BlockSpec((None, 1, d), lambda v, vt, ve, lo, hi: (ve[v], 0, 0))],
        out_specs=pl.BlockSpec((tr, half), lambda v, vt, ve, lo, hi: (vt[v], 0)),
    )
    return pl.pallas_call(
        _experts_kernel, grid_spec=grid_spec,
        out_shape=jax.ShapeDtypeStruct(hs.shape, jnp.uint32),
        compiler_params=_params(("arbitrary",), 48),
        name="moe_experts",
    )(vt, ve, vlo, vhi, hs, w_gu, b_gu.reshape(n_e, 1, f2), w_dn, b_dn.reshape(n_e, 1, d))


def _combine_kernel(dest_ref, ys_hbm, w_ref, x_ref, gate_ref, o_ref, buf_ref, row_sem):
    i = pl.program_id(0)
    last = pl.num_programs(0) - 1
    tm, d = x_ref.shape
    half = d // 2
    n_chunks = half // LANES
    slot = i % 2
    other = 1 - slot

    def row_copy(tile, sl, t, k):
        src = dest_ref[(tile * tm + t) * TOP_K + k]
        return pltpu.make_async_copy(ys_hbm.at[pl.ds(src, 1), :],
                                     buf_ref.at[sl, k, pl.ds(t, 1), :], row_sem.at[sl])

    def wait_rows(tile, sl):
        def drain(t, carry):
            for k in range(TOP_K):
                row_copy(tile, sl, t, k).wait()
            return carry
        lax.fori_loop(0, tm, drain, 0, unroll=2)

    @pl.when(i == 0)
    def _():
        def issue(t, carry):
            for k in range(TOP_K):
                row_copy(0, 0, t, k).start()
            return carry
        lax.fori_loop(0, tm, issue, 0, unroll=2)

    wait_rows(i, slot)

    nxt = jnp.minimum(i + 1, last)
    per_chunk = tm // n_chunks
    w = w_ref[...]
    wb = [jnp.broadcast_to(w[:, k:k + 1], (tm, LANES)) for k in range(TOP_K)]
    for c in range(n_chunks):
        lo = jnp.zeros((tm, LANES), F32)
        hi = jnp.zeros((tm, LANES), F32)
        for k in range(TOP_K):
            u = buf_ref[slot, k, :, c * LANES:(c + 1) * LANES]
            lo = lo + wb[k] * _unpack_lo(u)
            hi = hi + wb[k] * _unpack_hi(u)
        cl = slice(c * LANES, (c + 1) * LANES)
        ch = slice(half + c * LANES, half + (c + 1) * LANES)
        o_ref[:, cl] = x_ref[:, cl] + gate_ref[:, cl] * lo
        o_ref[:, ch] = x_ref[:, ch] + gate_ref[:, ch] * hi
        for t in range(c * per_chunk, (c + 1) * per_chunk):
            for k in range(TOP_K):
                row_copy(nxt, other, t, k).start()

    @pl.when(i == last)
    def _():
        wait_rows(last, other)


def _combine(ys, dest, w4, x, gate):
    s, d = x.shape
    tm = _tile(s, 256, 256)
    tok = lambda i, dest: (i, 0)
    grid_spec = pltpu.PrefetchScalarGridSpec(
        num_scalar_prefetch=1,
        grid=(s // tm,),
        in_specs=[pl.BlockSpec(memory_space=pl.ANY),
                  pl.BlockSpec((tm, TOP_K), tok), pl.BlockSpec((tm, d), tok),
                  pl.BlockSpec((1, d), lambda i, dest: (0, 0))],
        out_specs=pl.BlockSpec((tm, d), tok),
        scratch_shapes=[pltpu.VMEM((2, TOP_K, tm, d // 2), jnp.uint32),
                        pltpu.SemaphoreType.DMA((2,))],
    )
    return pl.pallas_call(
        _combine_kernel, grid_spec=grid_spec,
        out_shape=jax.ShapeDtypeStruct((s, d), F32),
        compiler_params=_params(("arbitrary",), 48),
        name="moe_combine",
    )(dest, ys, w4, x, gate)


def _expert_visits(off, m, tr, n_e):
    nt = m // tr
    pts = jnp.sort(jnp.concatenate([jnp.arange(nt, dtype=jnp.int32) * tr, off[:n_e]]))
    nxt = jnp.concatenate([pts[1:], jnp.full((1,), m, jnp.int32)])
    vt = jnp.minimum(pts // tr, nt - 1)
    ve = jnp.minimum(jnp.sum(off[None, 1:] <= pts[:, None], axis=1), n_e - 1).astype(jnp.int32)
    return vt, ve, pts - vt * tr, nxt - vt * tr


def _moe(x, g, sc, sh, gate, w_router, b_router, w_gu, b_gu, w_dn, b_dn):
    s, d = x.shape
    n_e = w_router.shape[1]
    m = TOP_K * s
    tr = _tile(m, EXPERT_TILE, 16)
    hp, idx4, w4, pos4, counts = _route(x, g, sc, sh, w_router, b_router)
    counts = counts.reshape(n_e).astype(jnp.int32)
    off = jnp.concatenate([jnp.zeros((1,), jnp.int32), jnp.cumsum(counts)])
    dest = (off[idx4] + pos4).reshape(m)
    hs = _dispatch(hp, dest)
    ys = _experts(hs, _expert_visits(off, m, tr, n_e), w_gu, b_gu, w_dn, b_dn, tr)
    return _combine(ys, dest, w4, x, gate)


def _prep_gu_kernel(w_ref, o_ref):
    n = w_ref.shape[1]
    src = lax.broadcasted_iota(jnp.int32, (n, n), 0)
    dst = lax.broadcasted_iota(jnp.int32, (n, n), 1)
    want = jnp.where(dst < n // 2, 2 * dst, 2 * (dst - n // 2) + 1)
    perm = jnp.where(src == want, 1.0, 0.0).astype(BF16)
    o_ref[...] = _dot(w_ref[...].astype(BF16), perm).astype(BF16)


def _prep_gu(w_gu):
    shape = w_gu.shape
    n = shape[-1]
    rows = w_gu.size // n
    tm = _tile(rows, 2048, 16)
    out = pl.pallas_call(
        _prep_gu_kernel, grid=(rows // tm,),
        in_specs=[pl.BlockSpec((tm, n), lambda i: (i, 0))],
        out_specs=pl.BlockSpec((tm, n), lambda i: (i, 0)),
        out_shape=jax.ShapeDtypeStruct((rows, n), BF16),
        compiler_params=_params(("arbitrary",), 32),
        name="prep_expert_up",
    )(w_gu.reshape(rows, n))
    return out.reshape(shape)


def _fcum_kernel(fl_ref, o_ref, carry_ref):
    @pl.when(pl.program_id(0) == 0)
    def _():
        carry_ref[...] = jnp.zeros(carry_ref.shape, F32)

    x = fl_ref[...]
    log_f = jnp.minimum(x, 0.0) - jnp.log1p(jnp.exp(-jnp.abs(x)))
    tm = x.shape[0]
    r = lax.broadcasted_iota(jnp.int32, (tm, tm), 0)
    c = lax.broadcasted_iota(jnp.int32, (tm, tm), 1)
    tri = jnp.where(c <= r, 1.0, 0.0).astype(BF16)
    hi, mid, lo = _split3(log_f)
    out = (_dot(tri, hi) + _dot(tri, mid) + _dot(tri, lo)) + carry_ref[...]
    o_ref[...] = out
    carry_ref[...] = out[tm - 1:tm, :]


def _forget_cumsum(fl):
    s, h = fl.shape
    tm = _tile(s, 256, 8)
    return pl.pallas_call(
        _fcum_kernel, grid=(s // tm,),
        in_specs=[pl.BlockSpec((tm, h), lambda i: (i, 0))],
        out_specs=pl.BlockSpec((tm, h), lambda i: (i, 0)),
        out_shape=jax.ShapeDtypeStruct((s, h), F32),
        scratch_shapes=[pltpu.VMEM((1, h), F32)],
        compiler_params=_params(("arbitrary",), 16),
        name="forget_cumsum",
    )(fl)


def _aug_kernel(src_ref, f_ref, g_ref, shift_ref, o_ref, *, mult, is_query):
    tm = src_ref.shape[0]
    n_heads = f_ref.shape[1]
    gm = g_ref[...] * mult
    lane = lax.broadcasted_iota(jnp.int32, (tm, HEAD_DIM), 1)
    for h in range(n_heads):
        t = src_ref[:, h * HEAD_DIM:(h + 1) * HEAD_DIM].astype(F32)
        ms = jnp.mean(t * t, axis=-1, keepdims=True)
        o_ref[:, h * AUG_DIM:h * AUG_DIM + HEAD_DIM] = (t * lax.rsqrt(ms + EPS) * gm).astype(BF16)
        f = jnp.broadcast_to(f_ref[:, h:h + 1] * LOG2E - shift_ref[...], (tm, HEAD_DIM))
        hi, mid, lo = [p.astype(F32) for p in _split3(f)]
        one = jnp.ones((tm, HEAD_DIM), F32)
        if is_query:
            pieces = (hi, mid, lo, one, one, one)
        else:
            pieces = (one, one, one, -hi, -mid, -lo)
        extra = jnp.zeros((tm, HEAD_DIM), F32)
        for p, piece in enumerate(pieces):
            extra = jnp.where(lane == p, piece, extra)
        o_ref[:, h * AUG_DIM + HEAD_DIM:(h + 1) * AUG_DIM] = extra.astype(BF16)


def _augment(src, f_cum, g, shift, *, mult, is_query):
    s = src.shape[0]
    n_heads = f_cum.shape[1]
    d = n_heads * HEAD_DIM
    tm = _tile(s, 256, 16)
    return pl.pallas_call(
        functools.partial(_aug_kernel, mult=mult, is_query=is_query),
        grid=(s // tm,),
        in_specs=[pl.BlockSpec((tm, d), lambda i: (i, 0)),
                  pl.BlockSpec((tm, n_heads), lambda i: (i, 0)),
                  pl.BlockSpec((1, HEAD_DIM), lambda i: (0, 0)),
                  pl.BlockSpec((1, 1), lambda i: (0, 0))],
        out_specs=pl.BlockSpec((tm, n_heads * AUG_DIM), lambda i: (i, 0)),
        out_shape=jax.ShapeDtypeStruct((s, n_heads * AUG_DIM), BF16),
        compiler_params=_params(("arbitrary",), 32),
        name="augment_q" if is_query else "augment_k",
    )(src, f_cum, g.reshape(1, HEAD_DIM), jnp.asarray(shift, F32).reshape(1, 1))


def _aug_v_kernel(v_ref, o_ref):
    tm = v_ref.shape[0]
    one = jnp.ones((tm, HEAD_DIM), BF16)
    for h in range(v_ref.shape[1] // HEAD_DIM):
        o_ref[:, h * AUG_DIM:h * AUG_DIM + HEAD_DIM] = v_ref[:, h * HEAD_DIM:(h + 1) * HEAD_DIM]
        o_ref[:, h * AUG_DIM + HEAD_DIM:(h + 1) * AUG_DIM] = one


def _augment_v(kv, d):
    s = kv.shape[0]
    n_heads = d // HEAD_DIM
    tm = _tile(s, 256, 16)
    return pl.pallas_call(
        _aug_v_kernel, grid=(s // tm,),
        in_specs=[pl.BlockSpec((tm, d), lambda i: (i, 1))],
        out_specs=pl.BlockSpec((tm, n_heads * AUG_DIM), lambda i: (i, 0)),
        out_shape=jax.ShapeDtypeStruct((s, n_heads * AUG_DIM), BF16),
        compiler_params=_params(("arbitrary",), 32),
        name="augment_v",
    )(kv)


def _attn_kernel(start_ref, q_ref, k_ref, v_ref, o_ref, m_ref, l_ref, acc_ref, *, tk):
    h = pl.program_id(0)
    qi = pl.program_id(1)
    n_sub = q_ref.shape[0] // tk
    m_ref[...] = jnp.full(m_ref.shape, MASK_VALUE, F32)
    l_ref[...] = jnp.zeros(l_ref.shape, F32)
    acc_ref[...] = jnp.zeros(acc_ref.shape, F32)

    def sub_step(a, kb, on_diagonal):
        qrows = slice(a * tk, (a + 1) * tk)
        krows = pl.ds(pl.multiple_of(kb * tk, tk), tk)
        s = lax.dot_general(q_ref[qrows, :], k_ref[krows, :], (((1,), (1,)), ((), ())),
                            preferred_element_type=F32)
        if on_diagonal:
            r = lax.broadcasted_iota(jnp.int32, s.shape, 0)
            c = lax.broadcasted_iota(jnp.int32, s.shape, 1)
            s = jnp.where(c <= r, s, MASK_VALUE)
        m_prev = m_ref[qrows, :]
        m_new = jnp.maximum(m_prev, jnp.max(s, axis=-1, keepdims=True))
        alpha = jnp.exp2(m_prev - m_new)
        p = jnp.exp2(s - m_new)
        l_ref[qrows, :] = alpha * l_ref[qrows, :] + jnp.sum(p, axis=-1, keepdims=True)
        acc_ref[qrows, :] = alpha * acc_ref[qrows, :] + _dot(p.astype(BF16), v_ref[krows, :])
        m_ref[qrows, :] = m_new

    def body(kb, carry):
        for a in range(n_sub):
            sub_step(a, kb, False)
        return carry

    first_diag = qi * n_sub
    lax.fori_loop(start_ref[h, qi], first_diag, body, 0)
    for a in range(n_sub):
        for b in range(a):
            sub_step(a, first_diag + b, False)
        sub_step(a, first_diag + a, True)
    o_ref[...] = (acc_ref[...] / l_ref[...]).astype(o_ref.dtype)


def _attention(q_aug, k_aug, kv, start, n_heads, tq, tk):
    s = q_aug.shape[0]
    nq = s // tq
    grid_spec = pltpu.PrefetchScalarGridSpec(
        num_scalar_prefetch=1,
        grid=(n_heads, nq),
        in_specs=[pl.BlockSpec((tq, AUG_DIM), lambda h, qi, st: (qi, h)),
                  pl.BlockSpec((s, AUG_DIM), lambda h, qi, st: (0, h)),
                  pl.BlockSpec((s, HEAD_DIM), lambda h, qi, st: (0, n_heads + h))],
        out_specs=pl.BlockSpec((tq, HEAD_DIM), lambda h, qi, st: (qi, h)),
        scratch_shapes=[pltpu.VMEM((tq, 1), F32), pltpu.VMEM((tq, 1), F32),
                        pltpu.VMEM((tq, HEAD_DIM), F32)],
    )
    return pl.pallas_call(
        functools.partial(_attn_kernel, tk=tk), grid_spec=grid_spec,
        out_shape=jax.ShapeDtypeStruct((s, n_heads * HEAD_DIM), BF16),
        compiler_params=_params(("arbitrary", "arbitrary"), 48),
        name="fox_attention",
    )(start, q_aug, k_aug, kv)


def _attn_bounded_kernel(start_ref, q_ref, k_ref, v_ref, o_ref, acc_ref, *, tk):
    h = pl.program_id(0)
    qi = pl.program_id(1)
    n_sub = q_ref.shape[0] // tk
    acc_ref[...] = jnp.zeros(acc_ref.shape, F32)

    def sub_step(a, kb, on_diagonal):
        qrows = slice(a * tk, (a + 1) * tk)
        krows = pl.ds(pl.multiple_of(kb * tk, tk), tk)
        s = lax.dot_general(q_ref[qrows, :], k_ref[krows, :], (((1,), (1,)), ((), ())),
                            preferred_element_type=F32)
        p = jnp.exp2(s)
        if on_diagonal:
            r = lax.broadcasted_iota(jnp.int32, s.shape, 0)
            c = lax.broadcasted_iota(jnp.int32, s.shape, 1)
            p = jnp.where(c <= r, p, 0.0)
        acc_ref[qrows, :] += _dot(p.astype(BF16), v_ref[krows, :])

    def body(kb, carry):
        for a in range(n_sub):
            sub_step(a, kb, False)
        return carry

    first_diag = qi * n_sub
    lax.fori_loop(start_ref[h, qi], first_diag, body, 0)
    for a in range(n_sub):
        for b in range(a):
            sub_step(a, first_diag + b, False)
        sub_step(a, first_diag + a, True)
    acc = acc_ref[...]
    o_ref[...] = (acc[:, :HEAD_DIM] / acc[:, HEAD_DIM:]).astype(o_ref.dtype)


def _attention_bounded(q_aug, k_aug, v_aug, start, n_heads, tq, tk):
    s = q_aug.shape[0]
    grid_spec = pltpu.PrefetchScalarGridSpec(
        num_scalar_prefetch=1,
        grid=(n_heads, s // tq),
        in_specs=[pl.BlockSpec((tq, AUG_DIM), lambda h, qi, st: (qi, h)),
                  pl.BlockSpec((s, AUG_DIM), lambda h, qi, st: (0, h)),
                  pl.BlockSpec((s, AUG_DIM), lambda h, qi, st: (0, h))],
        out_specs=pl.BlockSpec((tq, HEAD_DIM), lambda h, qi, st: (qi, h)),
        scratch_shapes=[pltpu.VMEM((tq, AUG_DIM), F32)],
    )
    return pl.pallas_call(
        functools.partial(_attn_bounded_kernel, tk=tk), grid_spec=grid_spec,
        out_shape=jax.ShapeDtypeStruct((s, n_heads * HEAD_DIM), BF16),
        compiler_params=_params(("arbitrary", "arbitrary"), 52),
        name="fox_attention_bounded",
    )(start, q_aug, k_aug, v_aug)


MAX_BOUNDED_LOGIT_RANGE = 80.0


def _logit_bound(q_norm_g, k_norm_g):
    return HEAD_DIM ** 0.5 * jnp.max(jnp.abs(q_norm_g * k_norm_g))


def _attention_start_blocks(f_cum, q_norm_g, k_norm_g, tq, tk):
    s, n_heads = f_cum.shape
    nq = s // tq
    nk = s // tk
    thresh = F32_EXP_UNDERFLOW + 2.0 * _logit_bound(q_norm_g, k_norm_g) + 1.0
    f_max_q = jnp.max(f_cum.reshape(nq, tq, n_heads), axis=1).T
    f_min_k = jnp.min(f_cum.reshape(nk, tk, n_heads), axis=1).T
    needed = (f_max_q[:, :, None] - f_min_k[:, None, :]) >= -thresh
    first_diag = jnp.arange(nq) * (tq // tk)
    needed = needed | (jnp.arange(nk)[None, None, :] >= first_diag[None, :, None])
    return jnp.argmax(needed, axis=-1).astype(jnp.int32)


def _gateout_kernel(o_ref, gp_ref, wo_ref, x_ref, gate_ref, out_ref, a_ref):
    @pl.when(pl.program_id(1) == 0)
    def _():
        a = o_ref[...].astype(F32) * jax.nn.sigmoid(gp_ref[...].astype(F32))
        a_ref[...] = a.astype(a_ref.dtype)

    out_ref[...] = x_ref[...] + gate_ref[...] * _dot(a_ref[...], wo_ref[...])


def _gate_out(o, qg, w_o, x, gate):
    s, d = x.shape
    tm = _tile(s, 512, 16)
    tn = _tile(d, WIDE_TN, 128)
    return pl.pallas_call(
        _gateout_kernel,
        grid=(s // tm, d // tn),
        in_specs=[pl.BlockSpec((tm, d), lambda i, j: (i, 0)),
                  pl.BlockSpec((tm, d), lambda i, j: (i, 1)),
                  pl.BlockSpec((d, tn), lambda i, j: (0, j)),
                  pl.BlockSpec((tm, tn), lambda i, j: (i, j)),
                  pl.BlockSpec((1, tn), lambda i, j: (0, j))],
        out_specs=pl.BlockSpec((tm, tn), lambda i, j: (i, j)),
        out_shape=jax.ShapeDtypeStruct((s, d), F32),
        scratch_shapes=[pltpu.VMEM((tm, d), BF16)],
        compiler_params=_params(("arbitrary", "arbitrary"), 48),
        name="gate_out",
    )(o, qg, w_o, x, gate)


ATTN_KV_BLOCK = 512
ATTN_Q_SUBTILES = 2


def kernel(x, c, w_ada, b_ada, ada_table, norm1_g, norm2_g, a_w_in, a_w_grp, a_b_grp, a_scale,
           a_w_out, kv_norm_g, w_kvf, b_f, k_norm_g, b_w_qg, q_norm_g, b_w_o, moe_w_router,
           moe_b_router, moe_w_gu, moe_b_gu, moe_w_dn, moe_b_dn):
    b, s, d = x.shape
    assert b == 1, "single-sequence trunk"
    depth = ada_table.shape[0]
    n_mod = ada_table.shape[1]
    n_a = a_w_in.shape[0]
    n_heads = d // HEAD_DIM
    tk = _tile(s, ATTN_KV_BLOCK, 16)
    tq = tk * ATTN_Q_SUBTILES if s % (tk * ATTN_Q_SUBTILES) == 0 else tk
    xs = x.reshape(s, d)

    mod = _ada(c, w_ada, b_ada).reshape(n_mod, d)
    zeros_row = jnp.zeros((1, d), F32)
    w_gu_all = _prep_gu(moe_w_gu)
    b_gu_all = jnp.concatenate([moe_b_gu[..., 0::2], moe_b_gu[..., 1::2]], axis=-1)
    k_aug = v_aug = kv = f_cum = None

    for layer in range(depth):
        m = mod + ada_table[layer]
        shift1, scale1, gate1, shift2, scale2, gate2 = [m[i:i + 1] for i in range(n_mod)]
        g1 = norm1_g[layer].reshape(1, d)
        g2 = norm2_g[layer].reshape(1, d)
        if layer < n_a:
            p = _in_proj_pool(xs, g1, scale1, shift1, a_w_in[layer].astype(BF16))
            xs = _pool_out(p, a_w_grp[layer].astype(BF16), a_b_grp[layer], a_scale[layer],
                           a_w_out[layer].astype(BF16), xs, gate1)
        else:
            j = layer - n_a
            qg = _norm_matmul(xs, g1, scale1, shift1, b_w_qg[j].astype(BF16))
            bound2 = LOG2E * (_logit_bound(q_norm_g[j], k_norm_g) + 1.0)
            bounded = 2.0 * bound2 <= MAX_BOUNDED_LOGIT_RANGE
            q_aug = _augment(qg, f_cum, q_norm_g[j], jnp.where(bounded, bound2, 0.0),
                             mult=HEAD_DIM ** -0.5 * LOG2E, is_query=True)
            start = _attention_start_blocks(f_cum, q_norm_g[j], k_norm_g, tq, tk)
            o = lax.cond(
                bounded,
                lambda qa, ka, va, kv_, st: _attention_bounded(qa, ka, va, st, n_heads, tq, tk),
                lambda qa, ka, va, kv_, st: _attention(qa, ka, kv_, st, n_heads, tq, tk),
                q_aug, k_aug, v_aug, kv, start)
            xs = _gate_out(o, qg, b_w_o[j].astype(BF16), xs, gate1)

        xs = _moe(xs, g2, scale2, shift2, gate2, moe_w_router[layer], moe_b_router[layer],
                  w_gu_all[layer], b_gu_all[layer], moe_w_dn[layer].astype(BF16), moe_b_dn[layer])

        if layer == n_a - 1:
            kv, fl = _norm_matmul(xs, kv_norm_g.reshape(1, d), zeros_row, zeros_row,
                                  w_kvf[:, :2 * d].astype(BF16),
                                  side=(w_kvf[:, 2 * d:].astype(BF16), b_f.reshape(1, n_heads)))
            f_cum = _forget_cumsum(fl)
            k_aug = _augment(kv, f_cum, k_norm_g, 0.0, mult=1.0, is_query=False)
            v_aug = _augment_v(kv, d)

    return xs.reshape(b, s, d)
```

```python
import functools

import jax
import jax.numpy as jnp
from jax import lax
from jax.experimental import pallas as pl
from jax.experimental.pallas import tpu as pltpu

F32 = jnp.float32
BF16 = jnp.bfloat16

HEAD_DIM = 128
POOL_WINDOWS = (2, 4, 8, 16)
TOP_K = 4
SWIGLU_LIMIT = 7.0
SWIGLU_ALPHA = 1.702
EPS = 1e-6
LOG2E = 1.4426950408889634

AUG_DIM = 2 * HEAD_DIM
F32_EXP_UNDERFLOW = 88.0
MASK_VALUE = -1e30
MIB = 1 << 20


def _tile(n, pref, align):
    if n <= pref:
        return n
    t = (pref // align) * align
    while t >= align:
        if n % t == 0:
            return t
        t -= align
    raise ValueError(f"no tile for {n} (pref {pref}, align {align})")


def _params(semantics, vmem_mib):
    return pltpu.CompilerParams(dimension_semantics=semantics, vmem_limit_bytes=vmem_mib * MIB)


def _split3(a):
    hi = a.astype(BF16)
    r1 = a - hi.astype(F32)
    mid = r1.astype(BF16)
    lo = (r1 - mid.astype(F32)).astype(BF16)
    return hi, mid, lo


def _dot(a, b):
    return jnp.dot(a, b, preferred_element_type=F32)


def _dot_f32ish(a, b):
    ah = a.astype(BF16)
    al = (a - ah.astype(F32)).astype(BF16)
    bh = b.astype(BF16)
    bl = (b - bh.astype(F32)).astype(BF16)
    return _dot(ah, bh) + _dot(al, bh) + _dot(ah, bl)


def _modulated_norm(x, gs, sh):
    ms = jnp.mean(x * x, axis=-1, keepdims=True)
    return x * lax.rsqrt(ms + EPS) * gs + sh


NORM_ROWS = 64
WIDE_TN = 1024


def _norm_rows_into(x_ref, g_ref, sc_ref, sh_ref, h_ref):
    tm = x_ref.shape[0]
    rows = min(NORM_ROWS, tm)
    gs = g_ref[...] * (1.0 + sc_ref[...])
    sh = sh_ref[...]

    def body(r, carry):
        sl = pl.ds(pl.multiple_of(r * rows, rows), rows)
        h_ref[sl, :] = _modulated_norm(x_ref[sl, :], gs, sh).astype(h_ref.dtype)
        return carry

    lax.fori_loop(0, tm // rows, body, 0)


def _ada_kernel(c_ref, w_ref, b_ref, o_ref):
    c = c_ref[...]
    s = c * jax.nn.sigmoid(c)
    o_ref[...] = jnp.sum(w_ref[...] * s, axis=0, keepdims=True) + b_ref[...]


def _ada(c, w_ada, b_ada):
    d, n = w_ada.shape
    tn = _tile(n, 512, 128)
    return pl.pallas_call(
        _ada_kernel,
        grid=(n // tn,),
        in_specs=[pl.BlockSpec((d, 1), lambda j: (0, 0)),
                  pl.BlockSpec((d, tn), lambda j: (0, j)),
                  pl.BlockSpec((1, tn), lambda j: (0, j))],
        out_specs=pl.BlockSpec((1, tn), lambda j: (0, j)),
        out_shape=jax.ShapeDtypeStruct((1, n), F32),
        compiler_params=_params(("arbitrary",), 40),
        name="ada_proj",
    )(c.reshape(d, 1), w_ada, b_ada.reshape(1, n))


def _nmm_kernel(x_ref, g_ref, sc_ref, sh_ref, w_ref, o_ref, h_ref):
    @pl.when(pl.program_id(1) == 0)
    def _():
        _norm_rows_into(x_ref, g_ref, sc_ref, sh_ref, h_ref)

    o_ref[...] = _dot(h_ref[...], w_ref[...]).astype(o_ref.dtype)


def _nmm_side_kernel(x_ref, g_ref, sc_ref, sh_ref, w_ref, ws_ref, bs_ref, o_ref, os_ref, h_ref):
    @pl.when(pl.program_id(1) == 0)
    def _():
        _norm_rows_into(x_ref, g_ref, sc_ref, sh_ref, h_ref)
        os_ref[...] = _dot(h_ref[...], ws_ref[...]) + bs_ref[...]

    o_ref[...] = _dot(h_ref[...], w_ref[...]).astype(o_ref.dtype)


def _norm_matmul(x, g, sc, sh, w_stack, layer, n, side=None):
    s, d = x.shape
    tm = _tile(s, 512, 16)
    tn = _tile(n, WIDE_TN, 128)
    row = lambda i, j: (0, 0)
    in_specs = [pl.BlockSpec((tm, d), lambda i, j: (i, 0)),
                pl.BlockSpec((1, d), row), pl.BlockSpec((1, d), row), pl.BlockSpec((1, d), row),
                pl.BlockSpec((None, d, tn), lambda i, j: (layer, 0, j))]
    out_spec = pl.BlockSpec((tm, tn), lambda i, j: (i, j))
    out_shape = jax.ShapeDtypeStruct((s, n), BF16)
    scratch = [pltpu.VMEM((tm, d), BF16)]
    cp = _params(("arbitrary", "arbitrary"), 48)
    if side is None:
        return pl.pallas_call(_nmm_kernel, grid=(s // tm, n // tn), in_specs=in_specs,
                              out_specs=out_spec, out_shape=out_shape, scratch_shapes=scratch,
                              compiler_params=cp, name="norm_matmul")(x, g, sc, sh, w_stack)
    ws, bs = side
    ns = ws.shape[1]
    in_specs += [pl.BlockSpec((d, ns), row), pl.BlockSpec((1, ns), row)]
    return pl.pallas_call(
        _nmm_side_kernel, grid=(s // tm, n // tn), in_specs=in_specs,
        out_specs=[out_spec, pl.BlockSpec((tm, ns), lambda i, j: (i, 0))],
        out_shape=[out_shape, jax.ShapeDtypeStruct((s, ns), F32)],
        scratch_shapes=scratch, compiler_params=cp, name="norm_matmul_side",
    )(x, g, sc, sh, w_stack, ws, bs)


POOL_HALO = 16


def _inpool_kernel(x_ref, g_ref, sc_ref, sh_ref, w_ref, o_ref, h_ref, u_ref, carry_ref, *,
                   cols_per_group):
    i = pl.program_id(0)
    j = pl.program_id(1)
    tm, tn = o_ref.shape

    @pl.when(j == 0)
    def _():
        _norm_rows_into(x_ref, g_ref, sc_ref, sh_ref, h_ref)

    u = _dot(h_ref[...], w_ref[...])

    @pl.when(i == 0)
    def _():
        u_ref[0:POOL_HALO, :] = jnp.zeros((POOL_HALO, tn), F32)

    @pl.when(i > 0)
    def _():
        u_ref[0:POOL_HALO, :] = carry_ref[j]

    u_ref[POOL_HALO:, :] = u
    carry_ref[j] = u[tm - POOL_HALO:, :]

    t1 = (i * tm + lax.broadcasted_iota(jnp.int32, (tm, 1), 0) + 1).astype(F32)
    group = (j * tn) // cols_per_group
    for gi, w in enumerate(POOL_WINDOWS):
        @pl.when(group == gi)
        def _(w=w):
            acc = u_ref[POOL_HALO:, :]
            for k in range(1, w):
                acc = acc + u_ref[POOL_HALO - k:POOL_HALO - k + tm, :]
            cnt = jnp.minimum(t1, float(w))
            o_ref[...] = (acc / cnt - u_ref[POOL_HALO:, :]).astype(o_ref.dtype)


def _in_proj_pool(x, g, sc, sh, w_in, layer):
    s, d = x.shape
    n = w_in.shape[2]
    cg = n // len(POOL_WINDOWS)
    tm = _tile(s, 512, 16)
    tn = _tile(cg, WIDE_TN, 128)
    row = lambda i, j: (0, 0)
    return pl.pallas_call(
        functools.partial(_inpool_kernel, cols_per_group=cg),
        grid=(s // tm, n // tn),
        in_specs=[pl.BlockSpec((tm, d), lambda i, j: (i, 0)),
                  pl.BlockSpec((1, d), row), pl.BlockSpec((1, d), row), pl.BlockSpec((1, d), row),
                  pl.BlockSpec((None, d, tn), lambda i, j: (layer, 0, j))],
        out_specs=pl.BlockSpec((tm, tn), lambda i, j: (i, j)),
        out_shape=jax.ShapeDtypeStruct((s, n), BF16),
        scratch_shapes=[pltpu.VMEM((tm, d), BF16),
                        pltpu.VMEM((POOL_HALO + tm, tn), F32),
                        pltpu.VMEM((n // tn, POOL_HALO, tn), F32)],
        compiler_params=_params(("arbitrary", "arbitrary"), 48),
        name="in_proj_pool",
    )(x, g, sc, sh, w_in)


def _poolout_kernel(p_ref, wg_ref, bg_ref, sg_ref, wo_ref, x_ref, gate_ref, o_ref, y_ref):
    @pl.when(pl.program_id(1) == 0)
    def _():
        n_groups, cg, _ = wg_ref.shape
        for gi in range(n_groups):
            cols = slice(gi * cg, (gi + 1) * cg)
            y = _dot(p_ref[:, cols], wg_ref[gi])
            y_ref[:, cols] = ((y + bg_ref[gi]) * sg_ref[gi]).astype(y_ref.dtype)

    o_ref[...] = x_ref[...] + gate_ref[...] * _dot(y_ref[...], wo_ref[...])


def _pool_out(p, w_grp, b_grp, scale, w_out, x, gate, layer):
    s, d = x.shape
    _, ng, cg, _ = w_grp.shape
    tm = _tile(s, 512, 16)
    tn = _tile(d, 512, 128)
    const3 = lambda i, j: (0, 0, 0)
    return pl.pallas_call(
        _poolout_kernel,
        grid=(s // tm, d // tn),
        in_specs=[pl.BlockSpec((tm, d), lambda i, j: (i, 0)),
                  pl.BlockSpec((None, ng, cg, cg), lambda i, j: (layer, 0, 0, 0)),
                  pl.BlockSpec((ng, 1, cg), const3),
                  pl.BlockSpec((ng, 1, cg), const3),
                  pl.BlockSpec((None, d, tn), lambda i, j: (layer, 0, j)),
                  pl.BlockSpec((tm, tn), lambda i, j: (i, j)),
                  pl.BlockSpec((1, tn), lambda i, j: (0, j))],
        out_specs=pl.BlockSpec((tm, tn), lambda i, j: (i, j)),
        out_shape=jax.ShapeDtypeStruct((s, d), F32),
        scratch_shapes=[pltpu.VMEM((tm, d), BF16)],
        compiler_params=_params(("arbitrary", "arbitrary"), 52),
        name="pool_out",
    )(p, w_grp, b_grp.reshape(ng, 1, cg), scale.reshape(ng, 1, cg), w_out, x, gate)


LANES = 128
HIGH_HALF = 0xFFFF0000


def _pack_rows(v):
    half = v.shape[1] // 2
    lo = lax.bitcast_convert_type(v[:, :half].astype(BF16).astype(F32), jnp.uint32)
    hi = lax.bitcast_convert_type(v[:, half:].astype(BF16).astype(F32), jnp.uint32)
    return (lo >> 16) | (hi & jnp.uint32(HIGH_HALF))


def _unpack_lo(u):
    return lax.bitcast_convert_type(u << 16, F32)


def _unpack_hi(u):
    return lax.bitcast_convert_type(u & jnp.uint32(HIGH_HALF), F32)


def _top_k(logits):
    n_e = logits.shape[-1]
    lane = lax.broadcasted_iota(jnp.int32, logits.shape, 1)
    work = logits
    vals, idxs = [], []
    for _ in range(TOP_K):
        m = jnp.max(work, axis=-1, keepdims=True)
        idx = jnp.min(jnp.where(work == m, lane, n_e), axis=-1, keepdims=True)
        vals.append(m)
        idxs.append(idx)
        work = jnp.where(lane == idx, -jnp.inf, work)
    return vals, idxs


def _route_kernel(x_ref, g_ref, sc_ref, sh_ref, wr_ref, br_ref,
                  hp_ref, idx_ref, w_ref, pos_ref, cnt_ref, lg_ref, carry_ref):
    tm, d = x_ref.shape
    n_e = wr_ref.shape[1]
    rows = min(NORM_ROWS, tm)

    @pl.when(pl.program_id(0) == 0)
    def _():
        carry_ref[...] = jnp.zeros(carry_ref.shape, F32)

    gs = g_ref[...] * (1.0 + sc_ref[...])
    sh = sh_ref[...]
    wr = wr_ref[...]
    wr_hi = wr.astype(BF16)
    wr_lo = (wr - wr_hi.astype(F32)).astype(BF16)
    for r in range(tm // rows):
        sl = slice(r * rows, (r + 1) * rows)
        h = _modulated_norm(x_ref[sl, :], gs, sh)
        h_hi = h.astype(BF16)
        h_lo = (h - h_hi.astype(F32)).astype(BF16)
        lg_ref[sl, :] = _dot(h_hi, wr_hi) + _dot(h_lo, wr_hi) + _dot(h_hi, wr_lo)
        hp_ref[sl, :] = _pack_rows(h)

    vals, idxs = _top_k(lg_ref[...] + br_ref[...])
    ex = [jnp.exp(v - vals[0]) for v in vals]
    den = ex[0]
    for e in ex[1:]:
        den = den + e

    lane = lax.broadcasted_iota(jnp.int32, (tm, n_e), 1)
    sel = jnp.zeros((tm, n_e), F32)
    for idx in idxs:
        sel = jnp.where(lane == idx, 1.0, sel)
    r_i = lax.broadcasted_iota(jnp.int32, (tm, tm), 0)
    c_i = lax.broadcasted_iota(jnp.int32, (tm, tm), 1)
    below = jnp.where(c_i < r_i, 1.0, 0.0).astype(BF16)
    ranks = _dot(below, sel.astype(BF16)) + carry_ref[...]

    slot = lax.broadcasted_iota(jnp.int32, (tm, TOP_K), 1)
    idx_out = jnp.zeros((tm, TOP_K), jnp.int32)
    w_out = jnp.zeros((tm, TOP_K), F32)
    pos_out = jnp.zeros((tm, TOP_K), F32)
    for k in range(TOP_K):
        pk = jnp.sum(jnp.where(lane == idxs[k], ranks, 0.0), axis=-1, keepdims=True)
        idx_out = jnp.where(slot == k, idxs[k], idx_out)
        w_out = jnp.where(slot == k, ex[k] / den, w_out)
        pos_out = jnp.where(slot == k, pk, pos_out)
    idx_ref[...] = idx_out
    w_ref[...] = w_out
    pos_ref[...] = pos_out.astype(jnp.int32)
    carry_ref[...] += jnp.sum(sel, axis=0, keepdims=True)
    cnt_ref[...] = carry_ref[...]


def _route(x, g, sc, sh, w_router, b_router):
    s, d = x.shape
    n_e = w_router.shape[1]
    half = d // 2
    tm = _tile(s, 512, 16)
    row = lambda i: (0, 0)
    tok = lambda i: (i, 0)
    return pl.pallas_call(
        _route_kernel,
        grid=(s // tm,),
        in_specs=[pl.BlockSpec((tm, d), tok),
                  pl.BlockSpec((1, d), row), pl.BlockSpec((1, d), row), pl.BlockSpec((1, d), row),
                  pl.BlockSpec((d, n_e), row), pl.BlockSpec((1, n_e), row)],
        out_specs=[pl.BlockSpec((tm, half), tok),
                   pl.BlockSpec((tm, TOP_K), tok), pl.BlockSpec((tm, TOP_K), tok),
                   pl.BlockSpec((tm, TOP_K), tok), pl.BlockSpec((1, n_e), row)],
        out_shape=[jax.ShapeDtypeStruct((s, half), jnp.uint32),
                   jax.ShapeDtypeStruct((s, TOP_K), jnp.int32),
                   jax.ShapeDtypeStruct((s, TOP_K), F32),
                   jax.ShapeDtypeStruct((s, TOP_K), jnp.int32),
                   jax.ShapeDtypeStruct((1, n_e), F32)],
        scratch_shapes=[pltpu.VMEM((tm, n_e), F32), pltpu.VMEM((1, n_e), F32)],
        compiler_params=_params(("arbitrary",), 40),
        name="moe_route",
    )(x, g, sc, sh, w_router, b_router.reshape(1, n_e))


def _dispatch_kernel(dest_ref, hp_ref, hs_hbm, row_sem):
    tm = hp_ref.shape[0]
    base = pl.program_id(0) * tm

    def row_copy(t, k):
        dst = dest_ref[(base + t) * TOP_K + k]
        return pltpu.make_async_copy(hp_ref.at[pl.ds(t, 1), :], hs_hbm.at[pl.ds(dst, 1), :], row_sem)

    def issue(t, carry):
        for k in range(TOP_K):
            row_copy(t, k).start()
        return carry

    def drain(t, carry):
        for k in range(TOP_K):
            row_copy(t, k).wait()
        return carry

    lax.fori_loop(0, tm, issue, 0, unroll=2)
    lax.fori_loop(0, tm, drain, 0, unroll=2)


def _dispatch(hp, dest):
    m = dest.shape[0]
    s, half = hp.shape
    tm = _tile(s, 256, 16)
    grid_spec = pltpu.PrefetchScalarGridSpec(
        num_scalar_prefetch=1,
        grid=(s // tm,),
        in_specs=[pl.BlockSpec((tm, half), lambda i, dest: (i, 0))],
        out_specs=pl.BlockSpec(memory_space=pl.ANY),
        scratch_shapes=[pltpu.SemaphoreType.DMA],
    )
    return pl.pallas_call(
        _dispatch_kernel, grid_spec=grid_spec,
        out_shape=jax.ShapeDtypeStruct((m, half), hp.dtype),
        compiler_params=_params(("arbitrary",), 16),
        name="moe_dispatch",
    )(dest, hp)


def _experts_kernel(vt_ref, ve_ref, lo_ref, hi_ref, hs_ref, wgu_ref, bgu_ref, wdn_ref, bdn_ref,
                    o_ref):
    v = pl.program_id(0)
    lo = lo_ref[v]
    hi = hi_ref[v]
    f, d = wdn_ref.shape
    tr, half = hs_ref.shape

    @pl.when(hi > lo)
    def _():
        hu = hs_ref[...]
        h_lo = _unpack_lo(hu).astype(BF16)
        h_hi = _unpack_hi(hu).astype(BF16)
        gu = _dot(h_lo, wgu_ref[:half, :]) + _dot(h_hi, wgu_ref[half:, :]) + bgu_ref[...]
        gt = jnp.minimum(gu[:, :f], SWIGLU_LIMIT)
        ln = jnp.clip(gu[:, f:], -SWIGLU_LIMIT, SWIGLU_LIMIT)
        act = (ln + 1.0) * gt * jax.nn.sigmoid(SWIGLU_ALPHA * gt)
        packed = _pack_rows(_dot(act.astype(BF16), wdn_ref[...]) + bdn_ref[...])
        row = lax.broadcasted_iota(jnp.int32, (tr, 1), 0)
        mine = (row >= lo) & (row < hi)

        @pl.when(lo == 0)
        def _():
            o_ref[...] = jnp.where(mine, packed, jnp.uint32(0))

        @pl.when(lo > 0)
        def _():
            o_ref[...] = jnp.where(mine, packed, o_ref[...])


EXPERT_TILE = 256


def _experts(hs, visits, w_gu, b_gu, w_dn, b_dn, tr, layer):
    vt, ve, vlo, vhi = visits
    _, n_e, d, f2 = w_gu.shape
    f = f2 // 2
    half = hs.shape[1]
    grid_spec = pltpu.PrefetchScalarGridSpec(
        num_scalar_prefetch=4,
        grid=(vt.shape[0],),
        in_specs=[pl.BlockSpec((tr, half), lambda v, vt, ve, lo, hi: (vt[v], 0)),
                  pl.BlockSpec((None, None, d, f2), lambda v, vt, ve, lo, hi: (layer, ve[v], 0, 0)),
                  pl.BlockSpec((None, 1, f2), lambda v, vt, ve, lo, hi: (ve[v], 0, 0)),
                  pl.BlockSpec((None, None, f, d), lambda v, vt, ve, lo, hi: (layer, ve[v], 0, 0)),
                  pl.BlockSpec((None, 1, d), lambda v, vt, ve, lo, hi: (ve[v], 0, 0))],
        out_specs=pl.BlockSpec((tr, half), lambda v, vt, ve, lo, hi: (vt[v], 0)),
    )
    return pl.pallas_call(
        _experts_kernel, grid_spec=grid_spec,
        out_shape=jax.ShapeDtypeStruct(hs.shape, jnp.uint32),
        compiler_params=_params(("arbitrary",), 48),
        name="moe_experts",
    )(vt, ve, vlo, vhi, hs, w_gu, b_gu.reshape(n_e, 1, f2), w_dn, b_dn.reshape(n_e, 1, d))


def _combine_kernel(dest_ref, ys_hbm, w_ref, x_ref, gate_ref, o_ref, buf_ref, row_sem):
    i = pl.program_id(0)
    last = pl.num_programs(0) - 1
    tm, d = x_ref.shape
    half = d // 2
    n_chunks = half // LANES
    slot = i % 2
    other = 1 - slot

    def row_copy(tile, sl, t, k):
        src = dest_ref[(tile * tm + t) * TOP_K + k]
        return pltpu.make_async_copy(ys_hbm.at[pl.ds(src, 1), :],
                                     buf_ref.at[sl, k, pl.ds(t, 1), :], row_sem.at[sl])

    def wait_rows(tile, sl):
        def drain(t, carry):
            for k in range(TOP_K):
                row_copy(tile, sl, t, k).wait()
            return carry
        lax.fori_loop(0, tm, drain, 0, unroll=2)

    @pl.when(i == 0)
    def _():
        def issue(t, carry):
            for k in range(TOP_K):
                row_copy(0, 0, t, k).start()
            return carry
        lax.fori_loop(0, tm, issue, 0, unroll=2)

    wait_rows(i, slot)

    nxt = jnp.minimum(i + 1, last)
    per_chunk = tm // n_chunks
    w = w_ref[...]
    wb = [jnp.broadcast_to(w[:, k:k + 1], (tm, LANES)) for k in range(TOP_K)]
    for c in range(n_chunks):
        lo = jnp.zeros((tm, LANES), F32)
        hi = jnp.zeros((tm, LANES), F32)
        for k in range(TOP_K):
            u = buf_ref[slot, k, :, c * LANES:(c + 1) * LANES]
            lo = lo + wb[k] * _unpack_lo(u)
            hi = hi + wb[k] * _unpack_hi(u)
        cl = slice(c * LANES, (c + 1) * LANES)
        ch = slice(half + c * LANES, half + (c + 1) * LANES)
        o_ref[:, cl] = x_ref[:, cl] + gate_ref[:, cl] * lo
        o_ref[:, ch] = x_ref[:, ch] + gate_ref[:, ch] * hi
        for t in range(c * per_chunk, (c + 1) * per_chunk):
            for k in range(TOP_K):
                row_copy(nxt, other, t, k).start()

    @pl.when(i == last)
    def _():
        wait_rows(last, other)


def _combine(ys, dest, w4, x, gate):
    s, d = x.shape
    tm = _tile(s, 256, 256)
    tok = lambda i, dest: (i, 0)
    grid_spec = pltpu.PrefetchScalarGridSpec(
        num_scalar_prefetch=1,
        grid=(s // tm,),
        in_specs=[pl.BlockSpec(memory_space=pl.ANY),
                  pl.BlockSpec((tm, TOP_K), tok), pl.BlockSpec((tm, d), tok),
                  pl.BlockSpec((1, d), lambda i, dest: (0, 0))],
        out_specs=pl.BlockSpec((tm, d), tok),
        scratch_shapes=[pltpu.VMEM((2, TOP_K, tm, d // 2), jnp.uint32),
                        pltpu.SemaphoreType.DMA((2,))],
    )
    return pl.pallas_call(
        _combine_kernel, grid_spec=grid_spec,
        out_shape=jax.ShapeDtypeStruct((s, d), F32),
        compiler_params=_params(("arbitrary",), 48),
        name="moe_combine",
    )(dest, ys, w4, x, gate)


def _expert_visits(off, m, tr, n_e):
    nt = m // tr
    pts = jnp.sort(jnp.concatenate([jnp.arange(nt, dtype=jnp.int32) * tr, off[:n_e]]))
    nxt = jnp.concatenate([pts[1:], jnp.full((1,), m, jnp.int32)])
    vt = jnp.minimum(pts // tr, nt - 1)
    ve = jnp.minimum(jnp.sum(off[None, 1:] <= pts[:, None], axis=1), n_e - 1).astype(jnp.int32)
    return vt, ve, pts - vt * tr, nxt - vt * tr


def _moe(x, g, sc, sh, gate, w_router, b_router, w_gu, b_gu, w_dn, b_dn, layer):
    s, d = x.shape
    n_e = w_router.shape[1]
    m = TOP_K * s
    tr = _tile(m, EXPERT_TILE, 16)
    hp, idx4, w4, pos4, counts = _route(x, g, sc, sh, w_router, b_router)
    counts = counts.reshape(n_e).astype(jnp.int32)
    off = jnp.concatenate([jnp.zeros((1,), jnp.int32), jnp.cumsum(counts)])
    dest = (off[idx4] + pos4).reshape(m)
    hs = _dispatch(hp, dest)
    ys = _experts(hs, _expert_visits(off, m, tr, n_e), w_gu, b_gu, w_dn, b_dn, tr, layer)
    return _combine(ys, dest, w4, x, gate)


def _prep_gu_kernel(w_ref, o_ref):
    n = w_ref.shape[1]
    src = lax.broadcasted_iota(jnp.int32, (n, n), 0)
    dst = lax.broadcasted_iota(jnp.int32, (n, n), 1)
    want = jnp.where(dst < n // 2, 2 * dst, 2 * (dst - n // 2) + 1)
    perm = jnp.where(src == want, 1.0, 0.0).astype(BF16)
    o_ref[...] = _dot(w_ref[...].astype(BF16), perm).astype(BF16)


def _prep_gu(w_gu):
    shape = w_gu.shape
    n = shape[-1]
    rows = w_gu.size // n
    tm = _tile(rows, 2048, 16)
    out = pl.pallas_call(
        _prep_gu_kernel, grid=(rows // tm,),
        in_specs=[pl.BlockSpec((tm, n), lambda i: (i, 0))],
        out_specs=pl.BlockSpec((tm, n), lambda i: (i, 0)),
        out_shape=jax.ShapeDtypeStruct((rows, n), BF16),
        compiler_params=_params(("arbitrary",), 32),
        name="prep_expert_up",
    )(w_gu.reshape(rows, n))
    return out.reshape(shape)


def _fcum_kernel(fl_ref, o_ref, carry_ref):
    @pl.when(pl.program_id(0) == 0)
    def _():
        carry_ref[...] = jnp.zeros(carry_ref.shape, F32)

    x = fl_ref[...]
    log_f = jnp.minimum(x, 0.0) - jnp.log1p(jnp.exp(-jnp.abs(x)))
    tm = x.shape[0]
    r = lax.broadcasted_iota(jnp.int32, (tm, tm), 0)
    c = lax.broadcasted_iota(jnp.int32, (tm, tm), 1)
    tri = jnp.where(c <= r, 1.0, 0.0).astype(BF16)
    hi, mid, lo = _split3(log_f)
    out = (_dot(tri, hi) + _dot(tri, mid) + _dot(tri, lo)) + carry_ref[...]
    o_ref[...] = out
    carry_ref[...] = out[tm - 1:tm, :]


def _forget_cumsum(fl):
    s, h = fl.shape
    tm = _tile(s, 256, 8)
    return pl.pallas_call(
        _fcum_kernel, grid=(s // tm,),
        in_specs=[pl.BlockSpec((tm, h), lambda i: (i, 0))],
        out_specs=pl.BlockSpec((tm, h), lambda i: (i, 0)),
        out_shape=jax.ShapeDtypeStruct((s, h), F32),
        scratch_shapes=[pltpu.VMEM((1, h), F32)],
        compiler_params=_params(("arbitrary",), 16),
        name="forget_cumsum",
    )(fl)


AUG_HEAD_CHUNK = 4
N_BIAS_PIECES = 3


def _aug_kernel(src_ref, f4_ref, g_ref, shift_ref, sel_ref, o_ref, *, mult, n_heads):
    tm = src_ref.shape[0]
    gm = g_ref[...] * mult
    f = f4_ref[...] * LOG2E - shift_ref[...]
    hi, mid, lo = [p.astype(F32) for p in _split3(f)]
    lane = lax.broadcasted_iota(jnp.int32, f.shape, 1)
    pieces = jnp.where(lane < n_heads, hi,
                       jnp.where(lane < 2 * n_heads, mid,
                                 jnp.where(lane < 3 * n_heads, lo, 1.0))).astype(BF16)
    chunk = min(AUG_HEAD_CHUNK, n_heads)
    for h0 in range(0, n_heads, chunk):
        bias = _dot(pieces, sel_ref[:, h0 * HEAD_DIM:(h0 + chunk) * HEAD_DIM])
        for h in range(h0, h0 + chunk):
            t = src_ref[:, h * HEAD_DIM:(h + 1) * HEAD_DIM].astype(F32)
            ms = jnp.mean(t * t, axis=-1, keepdims=True)
            o_ref[:, h * AUG_DIM:h * AUG_DIM + HEAD_DIM] = (t * lax.rsqrt(ms + EPS) * gm).astype(BF16)
            o_ref[:, h * AUG_DIM + HEAD_DIM:(h + 1) * AUG_DIM] = (
                bias[:, (h - h0) * HEAD_DIM:(h - h0 + 1) * HEAD_DIM].astype(BF16))


def _bias_selector(n_heads, is_query):
    row = jnp.arange(HEAD_DIM)[:, None]
    col = jnp.arange(n_heads * HEAD_DIM)[None, :]
    h, c = col // HEAD_DIM, col % HEAD_DIM
    ones_row = N_BIAS_PIECES * n_heads + h
    if is_query:
        piece = (c < N_BIAS_PIECES) & (row == c * n_heads + h)
        ones = (c >= N_BIAS_PIECES) & (c < 2 * N_BIAS_PIECES) & (row == ones_row)
        sel = jnp.where(piece | ones, 1.0, 0.0)
    else:
        ones = (c < N_BIAS_PIECES) & (row == ones_row)
        piece = ((c >= N_BIAS_PIECES) & (c < 2 * N_BIAS_PIECES)
                 & (row == (c - N_BIAS_PIECES) * n_heads + h))
        sel = jnp.where(ones, 1.0, 0.0) - jnp.where(piece, 1.0, 0.0)
    return sel.astype(BF16)


def _augment(src, f4, g, shift, *, n_heads, mult, is_query):
    assert (N_BIAS_PIECES + 1) * n_heads <= HEAD_DIM
    s = src.shape[0]
    d = n_heads * HEAD_DIM
    tm = _tile(s, 256, 16)
    const = lambda i: (0, 0)
    return pl.pallas_call(
        functools.partial(_aug_kernel, mult=mult, n_heads=n_heads),
        grid=(s // tm,),
        in_specs=[pl.BlockSpec((tm, d), lambda i: (i, 0)),
                  pl.BlockSpec((tm, HEAD_DIM), lambda i: (i, 0)),
                  pl.BlockSpec((1, HEAD_DIM), const),
                  pl.BlockSpec((1, 1), const),
                  pl.BlockSpec((HEAD_DIM, d), const)],
        out_specs=pl.BlockSpec((tm, n_heads * AUG_DIM), lambda i: (i, 0)),
        out_shape=jax.ShapeDtypeStruct((s, n_heads * AUG_DIM), BF16),
        compiler_params=_params(("arbitrary",), 32),
        name="augment_q" if is_query else "augment_k",
    )(src, f4, g.reshape(1, HEAD_DIM), jnp.asarray(shift, F32).reshape(1, 1),
      _bias_selector(n_heads, is_query))


def _aug_v_kernel(v_ref, o_ref):
    tm = v_ref.shape[0]
    one = jnp.ones((tm, HEAD_DIM), BF16)
    for h in range(v_ref.shape[1] // HEAD_DIM):
        o_ref[:, h * AUG_DIM:h * AUG_DIM + HEAD_DIM] = v_ref[:, h * HEAD_DIM:(h + 1) * HEAD_DIM]
        o_ref[:, h * AUG_DIM + HEAD_DIM:(h + 1) * AUG_DIM] = one


def _augment_v(kv, d):
    s = kv.shape[0]
    n_heads = d // HEAD_DIM
    tm = _tile(s, 256, 16)
    return pl.pallas_call(
        _aug_v_kernel, grid=(s // tm,),
        in_specs=[pl.BlockSpec((tm, d), lambda i: (i, 1))],
        out_specs=pl.BlockSpec((tm, n_heads * AUG_DIM), lambda i: (i, 0)),
        out_shape=jax.ShapeDtypeStruct((s, n_heads * AUG_DIM), BF16),
        compiler_params=_params(("arbitrary",), 32),
        name="augment_v",
    )(kv)


def _attn_kernel(start_ref, q_ref, k_ref, v_ref, o_ref, m_ref, l_ref, acc_ref, *, tk):
    h = pl.program_id(0)
    qi = pl.program_id(1)
    n_sub = q_ref.shape[0] // tk
    m_ref[...] = jnp.full(m_ref.shape, MASK_VALUE, F32)
    l_ref[...] = jnp.zeros(l_ref.shape, F32)
    acc_ref[...] = jnp.zeros(acc_ref.shape, F32)

    def sub_step(a, kb, on_diagonal):
        qrows = slice(a * tk, (a + 1) * tk)
        krows = pl.ds(pl.multiple_of(kb * tk, tk), tk)
        s = lax.dot_general(q_ref[qrows, :], k_ref[krows, :], (((1,), (1,)), ((), ())),
                            preferred_element_type=F32)
        if on_diagonal:
            r = lax.broadcasted_iota(jnp.int32, s.shape, 0)
            c = lax.broadcasted_iota(jnp.int32, s.shape, 1)
            s = jnp.where(c <= r, s, MASK_VALUE)
        m_prev = m_ref[qrows, :]
        m_new = jnp.maximum(m_prev, jnp.max(s, axis=-1, keepdims=True))
        alpha = jnp.exp2(m_prev - m_new)
        p = jnp.exp2(s - m_new)
        l_ref[qrows, :] = alpha * l_ref[qrows, :] + jnp.sum(p, axis=-1, keepdims=True)
        acc_ref[qrows, :] = alpha * acc_ref[qrows, :] + _dot(p.astype(BF16), v_ref[krows, :])
        m_ref[qrows, :] = m_new

    def body(kb, carry):
        for a in range(n_sub):
            sub_step(a, kb, False)
        return carry

    first_diag = qi * n_sub
    lax.fori_loop(start_ref[h, qi], first_diag, body, 0)
    for a in range(n_sub):
        for b in range(a):
            sub_step(a, first_diag + b, False)
        sub_step(a, first_diag + a, True)
    o_ref[...] = (acc_ref[...] / l_ref[...]).astype(o_ref.dtype)


def _attention(q_aug, k_aug, kv, start, n_heads, tq, tk):
    s = q_aug.shape[0]
    nq = s // tq
    grid_spec = pltpu.PrefetchScalarGridSpec(
        num_scalar_prefetch=1,
        grid=(n_heads, nq),
        in_specs=[pl.BlockSpec((tq, AUG_DIM), lambda h, qi, st: (qi, h)),
                  pl.BlockSpec((s, AUG_DIM), lambda h, qi, st: (0, h)),
                  pl.BlockSpec((s, HEAD_DIM), lambda h, qi, st: (0, n_heads + h))],
        out_specs=pl.BlockSpec((tq, HEAD_DIM), lambda h, qi, st: (qi, h)),
        scratch_shapes=[pltpu.VMEM((tq, 1), F32), pltpu.VMEM((tq, 1), F32),
                        pltpu.VMEM((tq, HEAD_DIM), F32)],
    )
    return pl.pallas_call(
        functools.partial(_attn_kernel, tk=tk), grid_spec=grid_spec,
        out_shape=jax.ShapeDtypeStruct((s, n_heads * HEAD_DIM), BF16),
        compiler_params=_params(("arbitrary", "arbitrary"), 48),
        name="fox_attention",
    )(start, q_aug, k_aug, kv)


def _attn_bounded_kernel(start_ref, q_ref, k_ref, v_ref, o_ref, acc_ref, *, tk):
    h = pl.program_id(0)
    qi = pl.program_id(1)
    n_sub = q_ref.shape[0] // tk
    acc_ref[...] = jnp.zeros(acc_ref.shape, F32)

    def sub_step(a, kb, on_diagonal):
        qrows = slice(a * tk, (a + 1) * tk)
        krows = pl.ds(pl.multiple_of(kb * tk, tk), tk)
        s = lax.dot_general(q_ref[qrows, :], k_ref[krows, :], (((1,), (1,)), ((), ())),
                            preferred_element_type=F32)
        p = jnp.exp2(s)
        if on_diagonal:
            r = lax.broadcasted_iota(jnp.int32, s.shape, 0)
            c = lax.broadcasted_iota(jnp.int32, s.shape, 1)
            p = jnp.where(c <= r, p, 0.0)
        acc_ref[qrows, :] += _dot(p.astype(BF16), v_ref[krows, :])

    def body(kb, carry):
        for a in range(n_sub):
            sub_step(a, kb, False)
        return carry

    first_diag = qi * n_sub
    lax.fori_loop(start_ref[h, qi], first_diag, body, 0)
    for a in range(n_sub):
        for b in range(a):
            sub_step(a, first_diag + b, False)
        sub_step(a, first_diag + a, True)
    acc = acc_ref[...]
    o_ref[...] = (acc[:, :HEAD_DIM] / acc[:, HEAD_DIM:]).astype(o_ref.dtype)


def _attention_bounded(q_aug, k_aug, v_aug, start, n_heads, tq, tk):
    s = q_aug.shape[0]
    grid_spec = pltpu.PrefetchScalarGridSpec(
        num_scalar_prefetch=1,
        grid=(n_heads, s // tq),
        in_specs=[pl.BlockSpec((tq, AUG_DIM), lambda h, qi, st: (qi, h)),
                  pl.BlockSpec((s, AUG_DIM), lambda h, qi, st: (0, h)),
                  pl.BlockSpec((s, AUG_DIM), lambda h, qi, st: (0, h))],
        out_specs=pl.BlockSpec((tq, HEAD_DIM), lambda h, qi, st: (qi, h)),
        scratch_shapes=[pltpu.VMEM((tq, AUG_DIM), F32)],
    )
    return pl.pallas_call(
        functools.partial(_attn_bounded_kernel, tk=tk), grid_spec=grid_spec,
        out_shape=jax.ShapeDtypeStruct((s, n_heads * HEAD_DIM), BF16),
        compiler_params=_params(("arbitrary", "arbitrary"), 52),
        name="fox_attention_bounded",
    )(start, q_aug, k_aug, v_aug)


MAX_BOUNDED_LOGIT_RANGE = 80.0


def _logit_bound(q_norm_g, k_norm_g):
    return HEAD_DIM ** 0.5 * jnp.max(jnp.abs(q_norm_g * k_norm_g))


def _attention_start_blocks(f_cum, q_norm_g, k_norm_g, tq, tk):
    s, n_heads = f_cum.shape
    nq = s // tq
    nk = s // tk
    thresh = F32_EXP_UNDERFLOW + 2.0 * _logit_bound(q_norm_g, k_norm_g) + 1.0
    f_max_q = jnp.max(f_cum.reshape(nq, tq, n_heads), axis=1).T
    f_min_k = jnp.min(f_cum.reshape(nk, tk, n_heads), axis=1).T
    needed = (f_max_q[:, :, None] - f_min_k[:, None, :]) >= -thresh
    first_diag = jnp.arange(nq) * (tq // tk)
    needed = needed | (jnp.arange(nk)[None, None, :] >= first_diag[None, :, None])
    return jnp.argmax(needed, axis=-1).astype(jnp.int32)


def _gateout_kernel(o_ref, gp_ref, wo_ref, x_ref, gate_ref, out_ref, a_ref):
    @pl.when(pl.program_id(1) == 0)
    def _():
        a = o_ref[...].astype(F32) * jax.nn.sigmoid(gp_ref[...].astype(F32))
        a_ref[...] = a.astype(a_ref.dtype)

    out_ref[...] = x_ref[...] + gate_ref[...] * _dot(a_ref[...], wo_ref[...])


def _gate_out(o, qg, w_o, x, gate, layer):
    s, d = x.shape
    tm = _tile(s, 512, 16)
    tn = _tile(d, WIDE_TN, 128)
    return pl.pallas_call(
        _gateout_kernel,
        grid=(s // tm, d // tn),
        in_specs=[pl.BlockSpec((tm, d), lambda i, j: (i, 0)),
                  pl.BlockSpec((tm, d), lambda i, j: (i, 1)),
                  pl.BlockSpec((None, d, tn), lambda i, j: (layer, 0, j)),
                  pl.BlockSpec((tm, tn), lambda i, j: (i, j)),
                  pl.BlockSpec((1, tn), lambda i, j: (0, j))],
        out_specs=pl.BlockSpec((tm, tn), lambda i, j: (i, j)),
        out_shape=jax.ShapeDtypeStruct((s, d), F32),
        scratch_shapes=[pltpu.VMEM((tm, d), BF16)],
        compiler_params=_params(("arbitrary", "arbitrary"), 48),
        name="gate_out",
    )(o, qg, w_o, x, gate)


ATTN_KV_BLOCK = 512
ATTN_Q_SUBTILES = 2


def kernel(x, c, w_ada, b_ada, ada_table, norm1_g, norm2_g, a_w_in, a_w_grp, a_b_grp, a_scale,
           a_w_out, kv_norm_g, w_kvf, b_f, k_norm_g, b_w_qg, q_norm_g, b_w_o, moe_w_router,
           moe_b_router, moe_w_gu, moe_b_gu, moe_w_dn, moe_b_dn):
    b, s, d = x.shape
    assert b == 1, "single-sequence trunk"
    depth = ada_table.shape[0]
    n_mod = ada_table.shape[1]
    n_a = a_w_in.shape[0]
    n_heads = d // HEAD_DIM
    tk = _tile(s, ATTN_KV_BLOCK, 16)
    tq = tk * ATTN_Q_SUBTILES if s % (tk * ATTN_Q_SUBTILES) == 0 else tk
    xs = x.reshape(s, d)

    mod = _ada(c, w_ada, b_ada).reshape(n_mod, d)
    zeros_row = jnp.zeros((1, d), F32)
    w_gu_all = _prep_gu(moe_w_gu)
    w_dn_all = moe_w_dn.astype(BF16)
    w_in_all = a_w_in.astype(BF16)
    w_grp_all = a_w_grp.astype(BF16)
    w_out_all = a_w_out.astype(BF16)
    w_qg_all = b_w_qg.astype(BF16)
    w_o_all = b_w_o.astype(BF16)
    w_kvf_b = w_kvf.astype(BF16).reshape(1, d, w_kvf.shape[1])
    b_gu_all = jnp.concatenate([moe_b_gu[..., 0::2], moe_b_gu[..., 1::2]], axis=-1)
    k_aug = v_aug = kv = f_cum = f4 = None

    for layer in range(depth):
        m = mod + ada_table[layer]
        shift1, scale1, gate1, shift2, scale2, gate2 = [m[i:i + 1] for i in range(n_mod)]
        g1 = norm1_g[layer].reshape(1, d)
        g2 = norm2_g[layer].reshape(1, d)
        if layer < n_a:
            p = _in_proj_pool(xs, g1, scale1, shift1, w_in_all, layer)
            xs = _pool_out(p, w_grp_all, a_b_grp[layer], a_scale[layer], w_out_all, xs, gate1, layer)
        else:
            j = layer - n_a
            qg = _norm_matmul(xs, g1, scale1, shift1, w_qg_all, j, 2 * d)
            bound2 = LOG2E * (_logit_bound(q_norm_g[j], k_norm_g) + 1.0)
            bounded = 2.0 * bound2 <= MAX_BOUNDED_LOGIT_RANGE
            q_aug = _augment(qg, f4, q_norm_g[j], jnp.where(bounded, bound2, 0.0),
                             n_heads=n_heads, mult=HEAD_DIM ** -0.5 * LOG2E, is_query=True)
            start = _attention_start_blocks(f_cum, q_norm_g[j], k_norm_g, tq, tk)
            o = lax.cond(
                bounded,
                lambda qa, ka, va, kv_, st: _attention_bounded(qa, ka, va, st, n_heads, tq, tk),
                lambda qa, ka, va, kv_, st: _attention(qa, ka, kv_, st, n_heads, tq, tk),
                q_aug, k_aug, v_aug, kv, start)
            xs = _gate_out(o, qg, w_o_all, xs, gate1, j)

        xs = _moe(xs, g2, scale2, shift2, gate2, moe_w_router[layer], moe_b_router[layer],
                  w_gu_all, b_gu_all[layer], w_dn_all, moe_b_dn[layer], layer)

        if layer == n_a - 1:
            kv, fl = _norm_matmul(xs, kv_norm_g.reshape(1, d), zeros_row, zeros_row,
                                  w_kvf_b, 0, 2 * d,
                                  side=(w_kvf_b[0, :, 2 * d:], b_f.reshape(1, n_heads)))
            f_cum = _forget_cumsum(fl)
            f4 = jnp.pad(jnp.tile(f_cum, (1, N_BIAS_PIECES + 1)),
                         ((0, 0), (0, HEAD_DIM - (N_BIAS_PIECES + 1) * n_heads)))
            k_aug = _augment(kv, f4, k_norm_g, 0.0, n_heads=n_heads, mult=1.0, is_query=False)
            v_aug = _augment_v(kv, d)

    return xs.reshape(b, s, d)
```

```python
import functools

import jax
import jax.numpy as jnp
from jax import lax
from jax.experimental import pallas as pl
from jax.experimental.pallas import tpu as pltpu

F32 = jnp.float32
BF16 = jnp.bfloat16

HEAD_DIM = 128
POOL_WINDOWS = (2, 4, 8, 16)
TOP_K = 4
SWIGLU_LIMIT = 7.0
SWIGLU_ALPHA = 1.702
EPS = 1e-6
LOG2E = 1.4426950408889634

AUG_DIM = 2 * HEAD_DIM
F32_EXP_UNDERFLOW = 88.0
MASK_VALUE = -1e30
MIB = 1 << 20


def _tile(n, pref, align):
    if n <= pref:
        return n
    t = (pref // align) * align
    while t >= align:
        if n % t == 0:
            return t
        t -= align
    raise ValueError(f"no tile for {n} (pref {pref}, align {align})")


def _params(semantics, vmem_mib):
    return pltpu.CompilerParams(dimension_semantics=semantics, vmem_limit_bytes=vmem_mib * MIB)


def _split3(a):
    hi = a.astype(BF16)
    r1 = a - hi.astype(F32)
    mid = r1.astype(BF16)
    lo = (r1 - mid.astype(F32)).astype(BF16)
    return hi, mid, lo


def _dot(a, b):
    return jnp.dot(a, b, preferred_element_type=F32)


def _dot_f32ish(a, b):
    ah = a.astype(BF16)
    al = (a - ah.astype(F32)).astype(BF16)
    bh = b.astype(BF16)
    bl = (b - bh.astype(F32)).astype(BF16)
    return _dot(ah, bh) + _dot(al, bh) + _dot(ah, bl)


def _modulated_norm(x, gs, sh):
    ms = jnp.mean(x * x, axis=-1, keepdims=True)
    return x * lax.rsqrt(ms + EPS) * gs + sh


NORM_ROWS = 64
WIDE_TN = 1024


def _norm_rows_into(x_ref, g_ref, sc_ref, sh_ref, h_ref):
    tm = x_ref.shape[0]
    rows = min(NORM_ROWS, tm)
    gs = g_ref[...] * (1.0 + sc_ref[...])
    sh = sh_ref[...]

    def body(r, carry):
        sl = pl.ds(pl.multiple_of(r * rows, rows), rows)
        h_ref[sl, :] = _modulated_norm(x_ref[sl, :], gs, sh).astype(h_ref.dtype)
        return carry

    lax.fori_loop(0, tm // rows, body, 0)


def _ada_kernel(c_ref, w_ref, b_ref, o_ref):
    c = c_ref[...]
    s = c * jax.nn.sigmoid(c)
    o_ref[...] = jnp.sum(w_ref[...] * s, axis=0, keepdims=True) + b_ref[...]


def _ada(c, w_ada, b_ada):
    d, n = w_ada.shape
    tn = _tile(n, 512, 128)
    return pl.pallas_call(
        _ada_kernel,
        grid=(n // tn,),
        in_specs=[pl.BlockSpec((d, 1), lambda j: (0, 0)),
                  pl.BlockSpec((d, tn), lambda j: (0, j)),
                  pl.BlockSpec((1, tn), lambda j: (0, j))],
        out_specs=pl.BlockSpec((1, tn), lambda j: (0, j)),
        out_shape=jax.ShapeDtypeStruct((1, n), F32),
        compiler_params=_params(("arbitrary",), 40),
        name="ada_proj",
    )(c.reshape(d, 1), w_ada, b_ada.reshape(1, n))


def _nmm_kernel(x_ref, g_ref, sc_ref, sh_ref, w_ref, o_ref, h_ref):
    @pl.when(pl.program_id(1) == 0)
    def _():
        _norm_rows_into(x_ref, g_ref, sc_ref, sh_ref, h_ref)

    o_ref[...] = _dot(h_ref[...], w_ref[...]).astype(o_ref.dtype)


def _nmm_side_kernel(x_ref, g_ref, sc_ref, sh_ref, w_ref, ws_ref, bs_ref, o_ref, os_ref, h_ref):
    @pl.when(pl.program_id(1) == 0)
    def _():
        _norm_rows_into(x_ref, g_ref, sc_ref, sh_ref, h_ref)
        os_ref[...] = _dot(h_ref[...], ws_ref[...]) + bs_ref[...]

    o_ref[...] = _dot(h_ref[...], w_ref[...]).astype(o_ref.dtype)


def _norm_matmul(x, g, sc, sh, w_stack, layer, n, side=None):
    s, d = x.shape
    tm = _tile(s, 512, 16)
    tn = _tile(n, WIDE_TN, 128)
    row = lambda i, j: (0, 0)
    in_specs = [pl.BlockSpec((tm, d), lambda i, j: (i, 0)),
                pl.BlockSpec((1, d), row), pl.BlockSpec((1, d), row), pl.BlockSpec((1, d), row),
                pl.BlockSpec((None, d, tn), lambda i, j: (layer, 0, j))]
    out_spec = pl.BlockSpec((tm, tn), lambda i, j: (i, j))
    out_shape = jax.ShapeDtypeStruct((s, n), BF16)
    scratch = [pltpu.VMEM((tm, d), BF16)]
    cp = _params(("arbitrary", "arbitrary"), 48)
    if side is None:
        return pl.pallas_call(_nmm_kernel, grid=(s // tm, n // tn), in_specs=in_specs,
                              out_specs=out_spec, out_shape=out_shape, scratch_shapes=scratch,
                              compiler_params=cp, name="norm_matmul")(x, g, sc, sh, w_stack)
    ws, bs = side
    ns = ws.shape[1]
    in_specs += [pl.BlockSpec((d, ns), row), pl.BlockSpec((1, ns), row)]
    return pl.pallas_call(
        _nmm_side_kernel, grid=(s // tm, n // tn), in_specs=in_specs,
        out_specs=[out_spec, pl.BlockSpec((tm, ns), lambda i, j: (i, 0))],
        out_shape=[out_shape, jax.ShapeDtypeStruct((s, ns), F32)],
        scratch_shapes=scratch, compiler_params=cp, name="norm_matmul_side",
    )(x, g, sc, sh, w_stack, ws, bs)


POOL_HALO = 16


def _inpool_kernel(x_ref, g_ref, sc_ref, sh_ref, w_ref, o_ref, h_ref, u_ref, carry_ref, *,
                   cols_per_group):
    i = pl.program_id(0)
    j = pl.program_id(1)
    tm, tn = o_ref.shape

    @pl.when(j == 0)
    def _():
        _norm_rows_into(x_ref, g_ref, sc_ref, sh_ref, h_ref)

    u = _dot(h_ref[...], w_ref[...])

    @pl.when(i == 0)
    def _():
        u_ref[0:POOL_HALO, :] = jnp.zeros((POOL_HALO, tn), F32)

    @pl.when(i > 0)
    def _():
        u_ref[0:POOL_HALO, :] = carry_ref[j]

    u_ref[POOL_HALO:, :] = u
    carry_ref[j] = u[tm - POOL_HALO:, :]

    t1 = (i * tm + lax.broadcasted_iota(jnp.int32, (tm, 1), 0) + 1).astype(F32)
    group = (j * tn) // cols_per_group
    for gi, w in enumerate(POOL_WINDOWS):
        @pl.when(group == gi)
        def _(w=w):
            acc = u_ref[POOL_HALO:, :]
            for k in range(1, w):
                acc = acc + u_ref[POOL_HALO - k:POOL_HALO - k + tm, :]
            cnt = jnp.minimum(t1, float(w))
            o_ref[...] = (acc / cnt - u_ref[POOL_HALO:, :]).astype(o_ref.dtype)


def _in_proj_pool(x, g, sc, sh, w_in, layer):
    s, d = x.shape
    n = w_in.shape[2]
    cg = n // len(POOL_WINDOWS)
    tm = _tile(s, 512, 16)
    tn = _tile(cg, WIDE_TN, 128)
    row = lambda i, j: (0, 0)
    return pl.pallas_call(
        functools.partial(_inpool_kernel, cols_per_group=cg),
        grid=(s // tm, n // tn),
        in_specs=[pl.BlockSpec((tm, d), lambda i, j: (i, 0)),
                  pl.BlockSpec((1, d), row), pl.BlockSpec((1, d), row), pl.BlockSpec((1, d), row),
                  pl.BlockSpec((None, d, tn), lambda i, j: (layer, 0, j))],
        out_specs=pl.BlockSpec((tm, tn), lambda i, j: (i, j)),
        out_shape=jax.ShapeDtypeStruct((s, n), BF16),
        scratch_shapes=[pltpu.VMEM((tm, d), BF16),
                        pltpu.VMEM((POOL_HALO + tm, tn), F32),
                        pltpu.VMEM((n // tn, POOL_HALO, tn), F32)],
        compiler_params=_params(("arbitrary", "arbitrary"), 48),
        name="in_proj_pool",
    )(x, g, sc, sh, w_in)


def _poolout_kernel(p_ref, wg_ref, bg_ref, sg_ref, wo_ref, x_ref, gate_ref, o_ref, y_ref):
    @pl.when(pl.program_id(1) == 0)
    def _():
        n_groups, cg, _ = wg_ref.shape
        for gi in range(n_groups):
            cols = slice(gi * cg, (gi + 1) * cg)
            y = _dot(p_ref[:, cols], wg_ref[gi])
            y_ref[:, cols] = ((y + bg_ref[gi]) * sg_ref[gi]).astype(y_ref.dtype)

    o_ref[...] = x_ref[...] + gate_ref[...] * _dot(y_ref[...], wo_ref[...])


def _pool_out(p, w_grp, b_grp, scale, w_out, x, gate, layer):
    s, d = x.shape
    _, ng, cg, _ = w_grp.shape
    tm = _tile(s, 512, 16)
    tn = _tile(d, 512, 128)
    const3 = lambda i, j: (0, 0, 0)
    return pl.pallas_call(
        _poolout_kernel,
        grid=(s // tm, d // tn),
        in_specs=[pl.BlockSpec((tm, d), lambda i, j: (i, 0)),
                  pl.BlockSpec((None, ng, cg, cg), lambda i, j: (layer, 0, 0, 0)),
                  pl.BlockSpec((ng, 1, cg), const3),
                  pl.BlockSpec((ng, 1, cg), const3),
                  pl.BlockSpec((None, d, tn), lambda i, j: (layer, 0, j)),
                  pl.BlockSpec((tm, tn), lambda i, j: (i, j)),
                  pl.BlockSpec((1, tn), lambda i, j: (0, j))],
        out_specs=pl.BlockSpec((tm, tn), lambda i, j: (i, j)),
        out_shape=jax.ShapeDtypeStruct((s, d), F32),
        scratch_shapes=[pltpu.VMEM((tm, d), BF16)],
        compiler_params=_params(("arbitrary", "arbitrary"), 52),
        name="pool_out",
    )(p, w_grp, b_grp.reshape(ng, 1, cg), scale.reshape(ng, 1, cg), w_out, x, gate)


LANES = 128
HIGH_HALF = 0xFFFF0000


def _pack_rows(v):
    half = v.shape[1] // 2
    lo = lax.bitcast_convert_type(v[:, :half].astype(BF16).astype(F32), jnp.uint32)
    hi = lax.bitcast_convert_type(v[:, half:].astype(BF16).astype(F32), jnp.uint32)
    return (lo >> 16) | (hi & jnp.uint32(HIGH_HALF))


def _unpack_lo(u):
    return lax.bitcast_convert_type(u << 16, F32)


def _unpack_hi(u):
    return lax.bitcast_convert_type(u & jnp.uint32(HIGH_HALF), F32)


def _top_k(logits):
    n_e = logits.shape[-1]
    lane = lax.broadcasted_iota(jnp.int32, logits.shape, 1)
    work = logits
    vals, idxs = [], []
    for _ in range(TOP_K):
        m = jnp.max(work, axis=-1, keepdims=True)
        idx = jnp.min(jnp.where(work == m, lane, n_e), axis=-1, keepdims=True)
        vals.append(m)
        idxs.append(idx)
        work = jnp.where(lane == idx, -jnp.inf, work)
    return vals, idxs


def _route_kernel(x_ref, g_ref, sc_ref, sh_ref, wr_ref, br_ref,
                  hp_ref, idx_ref, w_ref, pos_ref, cnt_ref, lg_ref, carry_ref):
    tm, d = x_ref.shape
    n_e = wr_ref.shape[1]
    rows = min(NORM_ROWS, tm)

    @pl.when(pl.program_id(0) == 0)
    def _():
        carry_ref[...] = jnp.zeros(carry_ref.shape, F32)

    gs = g_ref[...] * (1.0 + sc_ref[...])
    sh = sh_ref[...]
    wr = wr_ref[...]
    wr_hi = wr.astype(BF16)
    wr_lo = (wr - wr_hi.astype(F32)).astype(BF16)
    for r in range(tm // rows):
        sl = slice(r * rows, (r + 1) * rows)
        h = _modulated_norm(x_ref[sl, :], gs, sh)
        h_hi = h.astype(BF16)
        h_lo = (h - h_hi.astype(F32)).astype(BF16)
        lg_ref[sl, :] = _dot(h_hi, wr_hi) + _dot(h_lo, wr_hi) + _dot(h_hi, wr_lo)
        hp_ref[sl, :] = _pack_rows(h)

    vals, idxs = _top_k(lg_ref[...] + br_ref[...])
    ex = [jnp.exp(v - vals[0]) for v in vals]
    den = ex[0]
    for e in ex[1:]:
        den = den + e

    lane = lax.broadcasted_iota(jnp.int32, (tm, n_e), 1)
    sel = jnp.zeros((tm, n_e), F32)
    for idx in idxs:
        sel = jnp.where(lane == idx, 1.0, sel)
    r_i = lax.broadcasted_iota(jnp.int32, (tm, tm), 0)
    c_i = lax.broadcasted_iota(jnp.int32, (tm, tm), 1)
    below = jnp.where(c_i < r_i, 1.0, 0.0).astype(BF16)
    ranks = _dot(below, sel.astype(BF16)) + carry_ref[...]

    slot = lax.broadcasted_iota(jnp.int32, (tm, TOP_K), 1)
    idx_out = jnp.zeros((tm, TOP_K), jnp.int32)
    w_out = jnp.zeros((tm, TOP_K), F32)
    pos_out = jnp.zeros((tm, TOP_K), F32)
    for k in range(TOP_K):
        pk = jnp.sum(jnp.where(lane == idxs[k], ranks, 0.0), axis=-1, keepdims=True)
        idx_out = jnp.where(slot == k, idxs[k], idx_out)
        w_out = jnp.where(slot == k, ex[k] / den, w_out)
        pos_out = jnp.where(slot == k, pk, pos_out)
    idx_ref[...] = idx_out
    w_ref[...] = w_out
    pos_ref[...] = pos_out.astype(jnp.int32)
    carry_ref[...] += jnp.sum(sel, axis=0, keepdims=True)
    cnt_ref[...] = carry_ref[...]


def _route(x, g, sc, sh, w_router, b_router):
    s, d = x.shape
    n_e = w_router.shape[1]
    half = d // 2
    tm = _tile(s, 512, 16)
    row = lambda i: (0, 0)
    tok = lambda i: (i, 0)
    return pl.pallas_call(
        _route_kernel,
        grid=(s // tm,),
        in_specs=[pl.BlockSpec((tm, d), tok),
                  pl.BlockSpec((1, d), row), pl.BlockSpec((1, d), row), pl.BlockSpec((1, d), row),
                  pl.BlockSpec((d, n_e), row), pl.BlockSpec((1, n_e), row)],
        out_specs=[pl.BlockSpec((tm, half), tok),
                   pl.BlockSpec((tm, TOP_K), tok), pl.BlockSpec((tm, TOP_K), tok),
                   pl.BlockSpec((tm, TOP_K), tok), pl.BlockSpec((1, n_e), row)],
        out_shape=[jax.ShapeDtypeStruct((s, half), jnp.uint32),
                   jax.ShapeDtypeStruct((s, TOP_K), jnp.int32),
                   jax.ShapeDtypeStruct((s, TOP_K), F32),
                   jax.ShapeDtypeStruct((s, TOP_K), jnp.int32),
                   jax.ShapeDtypeStruct((1, n_e), F32)],
        scratch_shapes=[pltpu.VMEM((tm, n_e), F32), pltpu.VMEM((1, n_e), F32)],
        compiler_params=_params(("arbitrary",), 40),
        name="moe_route",
    )(x, g, sc, sh, w_router, b_router.reshape(1, n_e))


def _dispatch_kernel(dest_ref, hp_ref, hs_hbm, row_sem):
    tm = hp_ref.shape[0]
    base = pl.program_id(0) * tm

    def row_copy(t, k):
        dst = dest_ref[(base + t) * TOP_K + k]
        return pltpu.make_async_copy(hp_ref.at[pl.ds(t, 1), :], hs_hbm.at[pl.ds(dst, 1), :], row_sem)

    def issue(t, carry):
        for k in range(TOP_K):
            row_copy(t, k).start()
        return carry

    def drain(t, carry):
        for k in range(TOP_K):
            row_copy(t, k).wait()
        return carry

    lax.fori_loop(0, tm, issue, 0, unroll=2)
    lax.fori_loop(0, tm, drain, 0, unroll=2)


def _dispatch(hp, dest):
    m = dest.shape[0]
    s, half = hp.shape
    tm = _tile(s, 256, 16)
    grid_spec = pltpu.PrefetchScalarGridSpec(
        num_scalar_prefetch=1,
        grid=(s // tm,),
        in_specs=[pl.BlockSpec((tm, half), lambda i, dest: (i, 0))],
        out_specs=pl.BlockSpec(memory_space=pl.ANY),
        scratch_shapes=[pltpu.SemaphoreType.DMA],
    )
    return pl.pallas_call(
        _dispatch_kernel, grid_spec=grid_spec,
        out_shape=jax.ShapeDtypeStruct((m, half), hp.dtype),
        compiler_params=_params(("arbitrary",), 16),
        name="moe_dispatch",
    )(dest, hp)


def _experts_kernel(vt_ref, ve_ref, lo_ref, hi_ref, hs_ref, wgu_ref, bgu_ref, wdn_ref, bdn_ref,
                    o_ref):
    v = pl.program_id(0)
    lo = lo_ref[v]
    hi = hi_ref[v]
    f, d = wdn_ref.shape
    tr, half = hs_ref.shape

    @pl.when(hi > lo)
    def _():
        hu = hs_ref[...]
        h_lo = _unpack_lo(hu).astype(BF16)
        h_hi = _unpack_hi(hu).astype(BF16)
        gu = _dot(h_lo, wgu_ref[:half, :]) + _dot(h_hi, wgu_ref[half:, :]) + bgu_ref[...]
        gt = jnp.minimum(gu[:, :f], SWIGLU_LIMIT)
        ln = jnp.clip(gu[:, f:], -SWIGLU_LIMIT, SWIGLU_LIMIT)
        act = (ln + 1.0) * gt * jax.nn.sigmoid(SWIGLU_ALPHA * gt)
        packed = _pack_rows(_dot(act.astype(BF16), wdn_ref[...]) + bdn_ref[...])
        row = lax.broadcasted_iota(jnp.int32, (tr, 1), 0)
        mine = (row >= lo) & (row < hi)

        @pl.when(lo == 0)
        def _():
            o_ref[...] = jnp.where(mine, packed, jnp.uint32(0))

        @pl.when(lo > 0)
        def _():
            o_ref[...] = jnp.where(mine, packed, o_ref[...])


EXPERT_TILE = 256


def _experts(hs, visits, w_gu, b_gu, w_dn, b_dn, tr, layer):
    vt, ve, vlo, vhi = visits
    _, n_e, d, f2 = w_gu.shape
    f = f2 // 2
    half = hs.shape[1]
    grid_spec = pltpu.PrefetchScalarGridSpec(
        num_scalar_prefetch=4,
        grid=(vt.shape[0],),
        in_specs=[pl.BlockSpec((tr, half), lambda v, vt, ve, lo, hi: (vt[v], 0)),
                  pl.BlockSpec((None, None, d, f2), lambda v, vt, ve, lo, hi: (layer, ve[v], 0, 0)),
                  pl.BlockSpec((None, 1, f2), lambda v, vt, ve, lo, hi: (ve[v], 0, 0)),
                  pl.BlockSpec((None, None, f, d), lambda v, vt, ve, lo, hi: (layer, ve[v], 0, 0)),
                  pl.BlockSpec((None, 1, d), lambda v, vt, ve, lo, hi: (ve[v], 0, 0))],
        out_specs=pl.BlockSpec((tr, half), lambda v, vt, ve, lo, hi: (vt[v], 0)),
    )
    return pl.pallas_call(
        _experts_kernel, grid_spec=grid_spec,
        out_shape=jax.ShapeDtypeStruct(hs.shape, jnp.uint32),
        compiler_params=_params(("arbitrary",), 48),
        name="moe_experts",
    )(vt, ve, vlo, vhi, hs, w_gu, b_gu.reshape(n_e, 1, f2), w_dn, b_dn.reshape(n_e, 1, d))


def _combine_kernel(dest_ref, ys_hbm, w_ref, x_ref, gate_ref, o_ref, buf_ref, row_sem):
    i = pl.program_id(0)
    last = pl.num_programs(0) - 1
    tm, d = x_ref.shape
    half = d // 2
    n_chunks = half // LANES
    slot = i % 2
    other = 1 - slot

    def row_copy(tile, sl, t, k):
        src = dest_ref[(tile * tm + t) * TOP_K + k]
        return pltpu.make_async_copy(ys_hbm.at[pl.ds(src, 1), :],
                                     buf_ref.at[sl, k, pl.ds(t, 1), :], row_sem.at[sl])

    def wait_rows(tile, sl):
        def drain(t, carry):
            for k in range(TOP_K):
                row_copy(tile, sl, t, k).wait()
            return carry
        lax.fori_loop(0, tm, drain, 0, unroll=2)

    @pl.when(i == 0)
    def _():
        def issue(t, carry):
            for k in range(TOP_K):
                row_copy(0, 0, t, k).start()
            return carry
        lax.fori_loop(0, tm, issue, 0, unroll=2)

    wait_rows(i, slot)

    nxt = jnp.minimum(i + 1, last)
    per_chunk = tm // n_chunks
    w = w_ref[...]
    wb = [jnp.broadcast_to(w[:, k:k + 1], (tm, LANES)) for k in range(TOP_K)]
    for c in range(n_chunks):
        lo = jnp.zeros((tm, LANES), F32)
        hi = jnp.zeros((tm, LANES), F32)
        for k in range(TOP_K):
            u = buf_ref[slot, k, :, c * LANES:(c + 1) * LANES]
            lo = lo + wb[k] * _unpack_lo(u)
            hi = hi + wb[k] * _unpack_hi(u)
        cl = slice(c * LANES, (c + 1) * LANES)
        ch = slice(half + c * LANES, half + (c + 1) * LANES)
        o_ref[:, cl] = x_ref[:, cl] + gate_ref[:, cl] * lo
        o_ref[:, ch] = x_ref[:, ch] + gate_ref[:, ch] * hi
        for t in range(c * per_chunk, (c + 1) * per_chunk):
            for k in range(TOP_K):
                row_copy(nxt, other, t, k).start()

    @pl.when(i == last)
    def _():
        wait_rows(last, other)


def _combine(ys, dest, w4, x, gate):
    s, d = x.shape
    tm = _tile(s, 256, 256)
    tok = lambda i, dest: (i, 0)
    grid_spec = pltpu.PrefetchScalarGridSpec(
        num_scalar_prefetch=1,
        grid=(s // tm,),
        in_specs=[pl.BlockSpec(memory_space=pl.ANY),
                  pl.BlockSpec((tm, TOP_K), tok), pl.BlockSpec((tm, d), tok),
                  pl.BlockSpec((1, d), lambda i, dest: (0, 0))],
        out_specs=pl.BlockSpec((tm, d), tok),
        scratch_shapes=[pltpu.VMEM((2, TOP_K, tm, d // 2), jnp.uint32),
                        pltpu.SemaphoreType.DMA((2,))],
    )
    return pl.pallas_call(
        _combine_kernel, grid_spec=grid_spec,
        out_shape=jax.ShapeDtypeStruct((s, d), F32),
        compiler_params=_params(("arbitrary",), 48),
        name="moe_combine",
    )(dest, ys, w4, x, gate)


def _expert_visits(off, m, tr, n_e):
    nt = m // tr
    pts = jnp.sort(jnp.concatenate([jnp.arange(nt, dtype=jnp.int32) * tr, off[:n_e]]))
    nxt = jnp.concatenate([pts[1:], jnp.full((1,), m, jnp.int32)])
    vt = jnp.minimum(pts // tr, nt - 1)
    ve = jnp.minimum(jnp.sum(off[None, 1:] <= pts[:, None], axis=1), n_e - 1).astype(jnp.int32)
    return vt, ve, pts - vt * tr, nxt - vt * tr


def _moe(x, g, sc, sh, gate, w_router, b_router, w_gu, b_gu, w_dn, b_dn, layer):
    s, d = x.shape
    n_e = w_router.shape[1]
    m = TOP_K * s
    tr = _tile(m, EXPERT_TILE, 16)
    hp, idx4, w4, pos4, counts = _route(x, g, sc, sh, w_router, b_router)
    counts = counts.reshape(n_e).astype(jnp.int32)
    off = jnp.concatenate([jnp.zeros((1,), jnp.int32), jnp.cumsum(counts)])
    dest = (off[idx4] + pos4).reshape(m)
    hs = _dispatch(hp, dest)
    ys = _experts(hs, _expert_visits(off, m, tr, n_e), w_gu, b_gu, w_dn, b_dn, tr, layer)
    return _combine(ys, dest, w4, x, gate)


def _prep_gu_kernel(w_ref, o_ref):
    n = w_ref.shape[1]
    src = lax.broadcasted_iota(jnp.int32, (n, n), 0)
    dst = lax.broadcasted_iota(jnp.int32, (n, n), 1)
    want = jnp.where(dst < n // 2, 2 * dst, 2 * (dst - n // 2) + 1)
    perm = jnp.where(src == want, 1.0, 0.0).astype(BF16)
    o_ref[...] = _dot(w_ref[...].astype(BF16), perm).astype(BF16)


def _prep_gu(w_gu):
    shape = w_gu.shape
    n = shape[-1]
    rows = w_gu.size // n
    tm = _tile(rows, 2048, 16)
    out = pl.pallas_call(
        _prep_gu_kernel, grid=(rows // tm,),
        in_specs=[pl.BlockSpec((tm, n), lambda i: (i, 0))],
        out_specs=pl.BlockSpec((tm, n), lambda i: (i, 0)),
        out_shape=jax.ShapeDtypeStruct((rows, n), BF16),
        compiler_params=_params(("arbitrary",), 32),
        name="prep_expert_up",
    )(w_gu.reshape(rows, n))
    return out.reshape(shape)


def _fcum_kernel(fl_ref, o_ref, carry_ref):
    @pl.when(pl.program_id(0) == 0)
    def _():
        carry_ref[...] = jnp.zeros(carry_ref.shape, F32)

    x = fl_ref[...]
    log_f = jnp.minimum(x, 0.0) - jnp.log1p(jnp.exp(-jnp.abs(x)))
    tm = x.shape[0]
    r = lax.broadcasted_iota(jnp.int32, (tm, tm), 0)
    c = lax.broadcasted_iota(jnp.int32, (tm, tm), 1)
    tri = jnp.where(c <= r, 1.0, 0.0).astype(BF16)
    hi, mid, lo = _split3(log_f)
    out = (_dot(tri, hi) + _dot(tri, mid) + _dot(tri, lo)) + carry_ref[...]
    o_ref[...] = out
    carry_ref[...] = out[tm - 1:tm, :]


def _forget_cumsum(fl):
    s, h = fl.shape
    tm = _tile(s, 256, 8)
    return pl.pallas_call(
        _fcum_kernel, grid=(s // tm,),
        in_specs=[pl.BlockSpec((tm, h), lambda i: (i, 0))],
        out_specs=pl.BlockSpec((tm, h), lambda i: (i, 0)),
        out_shape=jax.ShapeDtypeStruct((s, h), F32),
        scratch_shapes=[pltpu.VMEM((1, h), F32)],
        compiler_params=_params(("arbitrary",), 16),
        name="forget_cumsum",
    )(fl)


AUG_HEAD_CHUNK = 4
N_BIAS_PIECES = 3


def _aug_kernel(src_ref, f4_ref, g_ref, shift_ref, sel_ref, o_ref, *, mult, n_heads):
    tm = src_ref.shape[0]
    gm = g_ref[...] * mult
    f = f4_ref[...] * LOG2E - shift_ref[...]
    hi, mid, lo = [p.astype(F32) for p in _split3(f)]
    lane = lax.broadcasted_iota(jnp.int32, f.shape, 1)
    pieces = jnp.where(lane < n_heads, hi,
                       jnp.where(lane < 2 * n_heads, mid,
                                 jnp.where(lane < 3 * n_heads, lo, 1.0))).astype(BF16)
    chunk = min(AUG_HEAD_CHUNK, n_heads)
    for h0 in range(0, n_heads, chunk):
        bias = _dot(pieces, sel_ref[:, h0 * HEAD_DIM:(h0 + chunk) * HEAD_DIM])
        for h in range(h0, h0 + chunk):
            t = src_ref[:, h * HEAD_DIM:(h + 1) * HEAD_DIM].astype(F32)
            ms = jnp.mean(t * t, axis=-1, keepdims=True)
            o_ref[:, h * AUG_DIM:h * AUG_DIM + HEAD_DIM] = (t * lax.rsqrt(ms + EPS) * gm).astype(BF16)
            o_ref[:, h * AUG_DIM + HEAD_DIM:(h + 1) * AUG_DIM] = (
                bias[:, (h - h0) * HEAD_DIM:(h - h0 + 1) * HEAD_DIM].astype(BF16))


def _bias_selector(n_heads, is_query):
    row = jnp.arange(HEAD_DIM)[:, None]
    col = jnp.arange(n_heads * HEAD_DIM)[None, :]
    h, c = col // HEAD_DIM, col % HEAD_DIM
    ones_row = N_BIAS_PIECES * n_heads + h
    if is_query:
        piece = (c < N_BIAS_PIECES) & (row == c * n_heads + h)
        ones = (c >= N_BIAS_PIECES) & (c < 2 * N_BIAS_PIECES) & (row == ones_row)
        sel = jnp.where(piece | ones, 1.0, 0.0)
    else:
        ones = (c < N_BIAS_PIECES) & (row == ones_row)
        piece = ((c >= N_BIAS_PIECES) & (c < 2 * N_BIAS_PIECES)
                 & (row == (c - N_BIAS_PIECES) * n_heads + h))
        sel = jnp.where(ones, 1.0, 0.0) - jnp.where(piece, 1.0, 0.0)
    return sel.astype(BF16)


def _augment(src, f4, g, shift, *, n_heads, mult, is_query):
    assert (N_BIAS_PIECES + 1) * n_heads <= HEAD_DIM
    s = src.shape[0]
    d = n_heads * HEAD_DIM
    tm = _tile(s, 256, 16)
    const = lambda i: (0, 0)
    return pl.pallas_call(
        functools.partial(_aug_kernel, mult=mult, n_heads=n_heads),
        grid=(s // tm,),
        in_specs=[pl.BlockSpec((tm, d), lambda i: (i, 0)),
                  pl.BlockSpec((tm, HEAD_DIM), lambda i: (i, 0)),
                  pl.BlockSpec((1, HEAD_DIM), const),
                  pl.BlockSpec((1, 1), const),
                  pl.BlockSpec((HEAD_DIM, d), const)],
        out_specs=pl.BlockSpec((tm, n_heads * AUG_DIM), lambda i: (i, 0)),
        out_shape=jax.ShapeDtypeStruct((s, n_heads * AUG_DIM), BF16),
        compiler_params=_params(("arbitrary",), 32),
        name="augment_q" if is_query else "augment_k",
    )(src, f4, g.reshape(1, HEAD_DIM), jnp.asarray(shift, F32).reshape(1, 1),
      _bias_selector(n_heads, is_query))


def _aug_v_kernel(v_ref, o_ref):
    tm = v_ref.shape[0]
    one = jnp.ones((tm, HEAD_DIM), BF16)
    for h in range(v_ref.shape[1] // HEAD_DIM):
        o_ref[:, h * AUG_DIM:h * AUG_DIM + HEAD_DIM] = v_ref[:, h * HEAD_DIM:(h + 1) * HEAD_DIM]
        o_ref[:, h * AUG_DIM + HEAD_DIM:(h + 1) * AUG_DIM] = one


def _augment_v(kv, d):
    s = kv.shape[0]
    n_heads = d // HEAD_DIM
    tm = _tile(s, 256, 16)
    return pl.pallas_call(
        _aug_v_kernel, grid=(s // tm,),
        in_specs=[pl.BlockSpec((tm, d), lambda i: (i, 1))],
        out_specs=pl.BlockSpec((tm, n_heads * AUG_DIM), lambda i: (i, 0)),
        out_shape=jax.ShapeDtypeStruct((s, n_heads * AUG_DIM), BF16),
        compiler_params=_params(("arbitrary",), 32),
        name="augment_v",
    )(kv)


def _attn_kernel(start_ref, q_ref, k_ref, v_ref, o_ref, m_ref, l_ref, acc_ref, *, tk):
    h = pl.program_id(0)
    qi = pl.program_id(1)
    n_sub = q_ref.shape[0] // tk
    m_ref[...] = jnp.full(m_ref.shape, MASK_VALUE, F32)
    l_ref[...] = jnp.zeros(l_ref.shape, F32)
    acc_ref[...] = jnp.zeros(acc_ref.shape, F32)

    def sub_step(a, kb, on_diagonal):
        qrows = slice(a * tk, (a + 1) * tk)
        krows = pl.ds(pl.multiple_of(kb * tk, tk), tk)
        s = lax.dot_general(q_ref[qrows, :], k_ref[krows, :], (((1,), (1,)), ((), ())),
                            preferred_element_type=F32)
        if on_diagonal:
            r = lax.broadcasted_iota(jnp.int32, s.shape, 0)
            c = lax.broadcasted_iota(jnp.int32, s.shape, 1)
            s = jnp.where(c <= r, s, MASK_VALUE)
        m_prev = m_ref[qrows, :]
        m_new = jnp.maximum(m_prev, jnp.max(s, axis=-1, keepdims=True))
        alpha = jnp.exp2(m_prev - m_new)
        p = jnp.exp2(s - m_new)
        l_ref[qrows, :] = alpha * l_ref[qrows, :] + jnp.sum(p, axis=-1, keepdims=True)
        acc_ref[qrows, :] = alpha * acc_ref[qrows, :] + _dot(p.astype(BF16), v_ref[krows, :])
        m_ref[qrows, :] = m_new

    def body(kb, carry):
        for a in range(n_sub):
            sub_step(a, kb, False)
        return carry

    first_diag = qi * n_sub
    lax.fori_loop(start_ref[h, qi], first_diag, body, 0)
    for a in range(n_sub):
        for b in range(a):
            sub_step(a, first_diag + b, False)
        sub_step(a, first_diag + a, True)
    o_ref[...] = (acc_ref[...] / l_ref[...]).astype(o_ref.dtype)


def _attention(q_aug, k_aug, kv, start, n_heads, tq, tk):
    s = q_aug.shape[0]
    nq = s // tq
    grid_spec = pltpu.PrefetchScalarGridSpec(
        num_scalar_prefetch=1,
        grid=(n_heads, nq),
        in_specs=[pl.BlockSpec((tq, AUG_DIM), lambda h, qi, st: (qi, h)),
                  pl.BlockSpec((s, AUG_DIM), lambda h, qi, st: (0, h)),
                  pl.BlockSpec((s, HEAD_DIM), lambda h, qi, st: (0, n_heads + h))],
        out_specs=pl.BlockSpec((tq, HEAD_DIM), lambda h, qi, st: (qi, h)),
        scratch_shapes=[pltpu.VMEM((tq, 1), F32), pltpu.VMEM((tq, 1), F32),
                        pltpu.VMEM((tq, HEAD_DIM), F32)],
    )
    return pl.pallas_call(
        functools.partial(_attn_kernel, tk=tk), grid_spec=grid_spec,
        out_shape=jax.ShapeDtypeStruct((s, n_heads * HEAD_DIM), BF16),
        compiler_params=_params(("arbitrary", "arbitrary"), 48),
        name="fox_attention",
    )(start, q_aug, k_aug, kv)


def _attn_bounded_kernel(start_ref, q_ref, k_ref, v_ref, o_ref, acc_ref, *, tk):
    h = pl.program_id(0)
    qi = pl.program_id(1)
    n_sub = q_ref.shape[0] // tk
    acc_ref[...] = jnp.zeros(acc_ref.shape, F32)

    def sub_step(a, kb, on_diagonal):
        qrows = slice(a * tk, (a + 1) * tk)
        krows = pl.ds(pl.multiple_of(kb * tk, tk), tk)
        s = lax.dot_general(q_ref[qrows, :], k_ref[krows, :], (((1,), (1,)), ((), ())),
                            preferred_element_type=F32)
        p = jnp.exp2(s)
        if on_diagonal:
            r = lax.broadcasted_iota(jnp.int32, s.shape, 0)
            c = lax.broadcasted_iota(jnp.int32, s.shape, 1)
            p = jnp.where(c <= r, p, 0.0)
        acc_ref[qrows, :] += _dot(p.astype(BF16), v_ref[krows, :])

    first_diag = qi * n_sub
    start = start_ref[h, qi]
    n_pairs = (first_diag - start) // 2

    def pair_body(j, carry):
        for kb in (start + 2 * j, start + 2 * j + 1):
            for a in range(n_sub):
                sub_step(a, kb, False)
        return carry

    lax.fori_loop(0, n_pairs, pair_body, 0)

    @pl.when(start + 2 * n_pairs < first_diag)
    def _():
        for a in range(n_sub):
            sub_step(a, first_diag - 1, False)

    for a in range(n_sub):
        for b in range(a):
            sub_step(a, first_diag + b, False)
        sub_step(a, first_diag + a, True)
    acc = acc_ref[...]
    o_ref[...] = (acc[:, :HEAD_DIM] / acc[:, HEAD_DIM:]).astype(o_ref.dtype)


def _attention_bounded(q_aug, k_aug, v_aug, start, n_heads, tq, tk):
    s = q_aug.shape[0]
    grid_spec = pltpu.PrefetchScalarGridSpec(
        num_scalar_prefetch=1,
        grid=(n_heads, s // tq),
        in_specs=[pl.BlockSpec((tq, AUG_DIM), lambda h, qi, st: (qi, h)),
                  pl.BlockSpec((s, AUG_DIM), lambda h, qi, st: (0, h)),
                  pl.BlockSpec((s, AUG_DIM), lambda h, qi, st: (0, h))],
        out_specs=pl.BlockSpec((tq, HEAD_DIM), lambda h, qi, st: (qi, h)),
        scratch_shapes=[pltpu.VMEM((tq, AUG_DIM), F32)],
    )
    return pl.pallas_call(
        functools.partial(_attn_bounded_kernel, tk=tk), grid_spec=grid_spec,
        out_shape=jax.ShapeDtypeStruct((s, n_heads * HEAD_DIM), BF16),
        compiler_params=_params(("arbitrary", "arbitrary"), 52),
        name="fox_attention_bounded",
    )(start, q_aug, k_aug, v_aug)


MAX_BOUNDED_LOGIT_RANGE = 80.0


def _logit_bound(q_norm_g, k_norm_g):
    return HEAD_DIM ** 0.5 * jnp.max(jnp.abs(q_norm_g * k_norm_g))


def _attention_start_blocks(f_cum, q_norm_g, k_norm_g, tq, tk):
    s, n_heads = f_cum.shape
    nq = s // tq
    nk = s // tk
    thresh = F32_EXP_UNDERFLOW + 2.0 * _logit_bound(q_norm_g, k_norm_g) + 1.0
    f_max_q = jnp.max(f_cum.reshape(nq, tq, n_heads), axis=1).T
    f_min_k = jnp.min(f_cum.reshape(nk, tk, n_heads), axis=1).T
    needed = (f_max_q[:, :, None] - f_min_k[:, None, :]) >= -thresh
    first_diag = jnp.arange(nq) * (tq // tk)
    needed = needed | (jnp.arange(nk)[None, None, :] >= first_diag[None, :, None])
    return jnp.argmax(needed, axis=-1).astype(jnp.int32)


def _gateout_kernel(o_ref, gp_ref, wo_ref, x_ref, gate_ref, out_ref, a_ref):
    @pl.when(pl.program_id(1) == 0)
    def _():
        a = o_ref[...].astype(F32) * jax.nn.sigmoid(gp_ref[...].astype(F32))
        a_ref[...] = a.astype(a_ref.dtype)

    out_ref[...] = x_ref[...] + gate_ref[...] * _dot(a_ref[...], wo_ref[...])


def _gate_out(o, qg, w_o, x, gate, layer):
    s, d = x.shape
    tm = _tile(s, 512, 16)
    tn = _tile(d, WIDE_TN, 128)
    return pl.pallas_call(
        _gateout_kernel,
        grid=(s // tm, d // tn),
        in_specs=[pl.BlockSpec((tm, d), lambda i, j: (i, 0)),
                  pl.BlockSpec((tm, d), lambda i, j: (i, 1)),
                  pl.BlockSpec((None, d, tn), lambda i, j: (layer, 0, j)),
                  pl.BlockSpec((tm, tn), lambda i, j: (i, j)),
                  pl.BlockSpec((1, tn), lambda i, j: (0, j))],
        out_specs=pl.BlockSpec((tm, tn), lambda i, j: (i, j)),
        out_shape=jax.ShapeDtypeStruct((s, d), F32),
        scratch_shapes=[pltpu.VMEM((tm, d), BF16)],
        compiler_params=_params(("arbitrary", "arbitrary"), 48),
        name="gate_out",
    )(o, qg, w_o, x, gate)


ATTN_KV_BLOCK = 512
ATTN_Q_SUBTILES = 2


def kernel(x, c, w_ada, b_ada, ada_table, norm1_g, norm2_g, a_w_in, a_w_grp, a_b_grp, a_scale,
           a_w_out, kv_norm_g, w_kvf, b_f, k_norm_g, b_w_qg, q_norm_g, b_w_o, moe_w_router,
           moe_b_router, moe_w_gu, moe_b_gu, moe_w_dn, moe_b_dn):
    b, s, d = x.shape
    assert b == 1, "single-sequence trunk"
    depth = ada_table.shape[0]
    n_mod = ada_table.shape[1]
    n_a = a_w_in.shape[0]
    n_heads = d // HEAD_DIM
    tk = _tile(s, ATTN_KV_BLOCK, 16)
    tq = tk * ATTN_Q_SUBTILES if s % (tk * ATTN_Q_SUBTILES) == 0 else tk
    xs = x.reshape(s, d)

    mod = _ada(c, w_ada, b_ada).reshape(n_mod, d)
    zeros_row = jnp.zeros((1, d), F32)
    w_gu_all = _prep_gu(moe_w_gu)
    w_dn_all = moe_w_dn.astype(BF16)
    w_in_all = a_w_in.astype(BF16)
    w_grp_all = a_w_grp.astype(BF16)
    w_out_all = a_w_out.astype(BF16)
    w_qg_all = b_w_qg.astype(BF16)
    w_o_all = b_w_o.astype(BF16)
    w_kvf_b = w_kvf.astype(BF16).reshape(1, d, w_kvf.shape[1])
    b_gu_all = jnp.concatenate([moe_b_gu[..., 0::2], moe_b_gu[..., 1::2]], axis=-1)
    k_aug = v_aug = kv = f_cum = f4 = None

    for layer in range(depth):
        m = mod + ada_table[layer]
        shift1, scale1, gate1, shift2, scale2, gate2 = [m[i:i + 1] for i in range(n_mod)]
        g1 = norm1_g[layer].reshape(1, d)
        g2 = norm2_g[layer].reshape(1, d)
        if layer < n_a:
            p = _in_proj_pool(xs, g1, scale1, shift1, w_in_all, layer)
            xs = _pool_out(p, w_grp_all, a_b_grp[layer], a_scale[layer], w_out_all, xs, gate1, layer)
        else:
            j = layer - n_a
            qg = _norm_matmul(xs, g1, scale1, shift1, w_qg_all, j, 2 * d)
            bound2 = LOG2E * (_logit_bound(q_norm_g[j], k_norm_g) + 1.0)
            bounded = 2.0 * bound2 <= MAX_BOUNDED_LOGIT_RANGE
            q_aug = _augment(qg, f4, q_norm_g[j], jnp.where(bounded, bound2, 0.0),
                             n_heads=n_heads, mult=HEAD_DIM ** -0.5 * LOG2E, is_query=True)
            start = _attention_start_blocks(f_cum, q_norm_g[j], k_norm_g, tq, tk)
            o = lax.cond(
                bounded,
                lambda qa, ka, va, kv_, st: _attention_bounded(qa, ka, va, st, n_heads, tq, tk),
                lambda qa, ka, va, kv_, st: _attention(qa, ka, kv_, st, n_heads, tq, tk),
                q_aug, k_aug, v_aug, kv, start)
            xs = _gate_out(o, qg, w_o_all, xs, gate1, j)

        xs = _moe(xs, g2, scale2, shift2, gate2, moe_w_router[layer], moe_b_router[layer],
                  w_gu_all, b_gu_all[layer], w_dn_all, moe_b_dn[layer], layer)

        if layer == n_a - 1:
            kv, fl = _norm_matmul(xs, kv_norm_g.reshape(1, d), zeros_row, zeros_row,
                                  w_kvf_b, 0, 2 * d,
                                  side=(w_kvf_b[0, :, 2 * d:], b_f.reshape(1, n_heads)))
            f_cum = _forget_cumsum(fl)
            f4 = jnp.pad(jnp.tile(f_cum, (1, N_BIAS_PIECES + 1)),
                         ((0, 0), (0, HEAD_DIM - (N_BIAS_PIECES + 1) * n_heads)))
            k_aug = _augment(kv, f4, k_norm_g, 0.0, n_heads=n_heads, mult=1.0, is_query=False)
            v_aug = _augment_v(kv, d)

    return xs.reshape(b, s, d)
```

```python
import functools

import jax
import jax.numpy as jnp
from jax import lax
from jax.experimental import pallas as pl
from jax.experimental.pallas import tpu as pltpu

F32 = jnp.float32
BF16 = jnp.bfloat16

HEAD_DIM = 128
POOL_WINDOWS = (2, 4, 8, 16)
TOP_K = 4
SWIGLU_LIMIT = 7.0
SWIGLU_ALPHA = 1.702
EPS = 1e-6
LOG2E = 1.4426950408889634

AUG_DIM = 2 * HEAD_DIM
F32_EXP_UNDERFLOW = 88.0
MASK_VALUE = -1e30
MIB = 1 << 20


def _tile(n, pref, align):
    if n <= pref:
        return n
    t = (pref // align) * align
    while t >= align:
        if n % t == 0:
            return t
        t -= align
    raise ValueError(f"no tile for {n} (pref {pref}, align {align})")


def _params(semantics, vmem_mib):
    return pltpu.CompilerParams(dimension_semantics=semantics, vmem_limit_bytes=vmem_mib * MIB)


def _split3(a):
    hi = a.astype(BF16)
    r1 = a - hi.astype(F32)
    mid = r1.astype(BF16)
    lo = (r1 - mid.astype(F32)).astype(BF16)
    return hi, mid, lo


def _dot(a, b):
    return jnp.dot(a, b, preferred_element_type=F32)


def _dot_f32ish(a, b):
    ah = a.astype(BF16)
    al = (a - ah.astype(F32)).astype(BF16)
    bh = b.astype(BF16)
    bl = (b - bh.astype(F32)).astype(BF16)
    return _dot(ah, bh) + _dot(al, bh) + _dot(ah, bl)


def _modulated_norm(x, gs, sh):
    ms = jnp.mean(x * x, axis=-1, keepdims=True)
    return x * lax.rsqrt(ms + EPS) * gs + sh


NORM_ROWS = 64
WIDE_TN = 1024


def _norm_rows_into(x_ref, g_ref, sc_ref, sh_ref, h_ref):
    tm = x_ref.shape[0]
    rows = min(NORM_ROWS, tm)
    gs = g_ref[...] * (1.0 + sc_ref[...])
    sh = sh_ref[...]

    def body(r, carry):
        sl = pl.ds(pl.multiple_of(r * rows, rows), rows)
        h_ref[sl, :] = _modulated_norm(x_ref[sl, :], gs, sh).astype(h_ref.dtype)
        return carry

    lax.fori_loop(0, tm // rows, body, 0)


def _ada_kernel(c_ref, w_ref, b_ref, o_ref):
    c = c_ref[...]
    s = c * jax.nn.sigmoid(c)
    o_ref[...] = jnp.sum(w_ref[...] * s, axis=0, keepdims=True) + b_ref[...]


def _ada(c, w_ada, b_ada):
    d, n = w_ada.shape
    tn = _tile(n, 512, 128)
    return pl.pallas_call(
        _ada_kernel,
        grid=(n // tn,),
        in_specs=[pl.BlockSpec((d, 1), lambda j: (0, 0)),
                  pl.BlockSpec((d, tn), lambda j: (0, j)),
                  pl.BlockSpec((1, tn), lambda j: (0, j))],
        out_specs=pl.BlockSpec((1, tn), lambda j: (0, j)),
        out_shape=jax.ShapeDtypeStruct((1, n), F32),
        compiler_params=_params(("arbitrary",), 40),
        name="ada_proj",
    )(c.reshape(d, 1), w_ada, b_ada.reshape(1, n))


def _nmm_kernel(x_ref, g_ref, sc_ref, sh_ref, w_ref, o_ref, h_ref):
    @pl.when(pl.program_id(1) == 0)
    def _():
        _norm_rows_into(x_ref, g_ref, sc_ref, sh_ref, h_ref)

    o_ref[...] = _dot(h_ref[...], w_ref[...]).astype(o_ref.dtype)


def _nmm_side_kernel(x_ref, g_ref, sc_ref, sh_ref, w_ref, ws_ref, bs_ref, o_ref, os_ref, h_ref):
    @pl.when(pl.program_id(1) == 0)
    def _():
        _norm_rows_into(x_ref, g_ref, sc_ref, sh_ref, h_ref)
        os_ref[...] = _dot(h_ref[...], ws_ref[...]) + bs_ref[...]

    o_ref[...] = _dot(h_ref[...], w_ref[...]).astype(o_ref.dtype)


def _norm_matmul(x, g, sc, sh, w_stack, layer, n, side=None):
    s, d = x.shape
    tm = _tile(s, 512, 16)
    tn = _tile(n, WIDE_TN, 128)
    row = lambda i, j: (0, 0)
    in_specs = [pl.BlockSpec((tm, d), lambda i, j: (i, 0)),
                pl.BlockSpec((1, d), row), pl.BlockSpec((1, d), row), pl.BlockSpec((1, d), row),
                pl.BlockSpec((None, d, tn), lambda i, j: (layer, 0, j))]
    out_spec = pl.BlockSpec((tm, tn), lambda i, j: (i, j))
    out_shape = jax.ShapeDtypeStruct((s, n), BF16)
    scratch = [pltpu.VMEM((tm, d), BF16)]
    cp = _params(("arbitrary", "arbitrary"), 48)
    if side is None:
        return pl.pallas_call(_nmm_kernel, grid=(s // tm, n // tn), in_specs=in_specs,
                              out_specs=out_spec, out_shape=out_shape, scratch_shapes=scratch,
                              compiler_params=cp, name="norm_matmul")(x, g, sc, sh, w_stack)
    ws, bs = side
    ns = ws.shape[1]
    in_specs += [pl.BlockSpec((d, ns), row), pl.BlockSpec((1, ns), row)]
    return pl.pallas_call(
        _nmm_side_kernel, grid=(s // tm, n // tn), in_specs=in_specs,
        out_specs=[out_spec, pl.BlockSpec((tm, ns), lambda i, j: (i, 0))],
        out_shape=[out_shape, jax.ShapeDtypeStruct((s, ns), F32)],
        scratch_shapes=scratch, compiler_params=cp, name="norm_matmul_side",
    )(x, g, sc, sh, w_stack, ws, bs)


POOL_HALO = 16


def _inpool_kernel(x_ref, g_ref, sc_ref, sh_ref, w_ref, o_ref, h_ref, u_ref, carry_ref, *,
                   cols_per_group):
    i = pl.program_id(0)
    j = pl.program_id(1)
    tm, tn = o_ref.shape

    @pl.when(j == 0)
    def _():
        _norm_rows_into(x_ref, g_ref, sc_ref, sh_ref, h_ref)

    u = _dot(h_ref[...], w_ref[...])

    @pl.when(i == 0)
    def _():
        u_ref[0:POOL_HALO, :] = jnp.zeros((POOL_HALO, tn), F32)

    @pl.when(i > 0)
    def _():
        u_ref[0:POOL_HALO, :] = carry_ref[j]

    u_ref[POOL_HALO:, :] = u
    carry_ref[j] = u[tm - POOL_HALO:, :]

    t1 = (i * tm + lax.broadcasted_iota(jnp.int32, (tm, 1), 0) + 1).astype(F32)
    group = (j * tn) // cols_per_group
    for gi, w in enumerate(POOL_WINDOWS):
        @pl.when(group == gi)
        def _(w=w):
            acc = u_ref[POOL_HALO:, :]
            for k in range(1, w):
                acc = acc + u_ref[POOL_HALO - k:POOL_HALO - k + tm, :]
            cnt = jnp.minimum(t1, float(w))
            o_ref[...] = (acc / cnt - u_ref[POOL_HALO:, :]).astype(o_ref.dtype)


def _in_proj_pool(x, g, sc, sh, w_in, layer):
    s, d = x.shape
    n = w_in.shape[2]
    cg = n // len(POOL_WINDOWS)
    tm = _tile(s, 512, 16)
    tn = _tile(cg, WIDE_TN, 128)
    row = lambda i, j: (0, 0)
    return pl.pallas_call(
        functools.partial(_inpool_kernel, cols_per_group=cg),
        grid=(s // tm, n // tn),
        in_specs=[pl.BlockSpec((tm, d), lambda i, j: (i, 0)),
                  pl.BlockSpec((1, d), row), pl.BlockSpec((1, d), row), pl.BlockSpec((1, d), row),
                  pl.BlockSpec((None, d, tn), lambda i, j: (layer, 0, j))],
        out_specs=pl.BlockSpec((tm, tn), lambda i, j: (i, j)),
        out_shape=jax.ShapeDtypeStruct((s, n), BF16),
        scratch_shapes=[pltpu.VMEM((tm, d), BF16),
                        pltpu.VMEM((POOL_HALO + tm, tn), F32),
                        pltpu.VMEM((n // tn, POOL_HALO, tn), F32)],
        compiler_params=_params(("arbitrary", "arbitrary"), 48),
        name="in_proj_pool",
    )(x, g, sc, sh, w_in)


def _poolout_kernel(p_ref, wg_ref, bg_ref, sg_ref, wo_ref, x_ref, gate_ref, o_ref, y_ref):
    @pl.when(pl.program_id(1) == 0)
    def _():
        n_groups, cg, _ = wg_ref.shape
        for gi in range(n_groups):
            cols = slice(gi * cg, (gi + 1) * cg)
            y = _dot(p_ref[:, cols], wg_ref[gi])
            y_ref[:, cols] = ((y + bg_ref[gi]) * sg_ref[gi]).astype(y_ref.dtype)

    o_ref[...] = x_ref[...] + gate_ref[...] * _dot(y_ref[...], wo_ref[...])


def _pool_out(p, w_grp, b_grp, scale, w_out, x, gate, layer):
    s, d = x.shape
    _, ng, cg, _ = w_grp.shape
    tm = _tile(s, 512, 16)
    tn = _tile(d, 512, 128)
    const3 = lambda i, j: (0, 0, 0)
    return pl.pallas_call(
        _poolout_kernel,
        grid=(s // tm, d // tn),
        in_specs=[pl.BlockSpec((tm, d), lambda i, j: (i, 0)),
                  pl.BlockSpec((None, ng, cg, cg), lambda i, j: (layer, 0, 0, 0)),
                  pl.BlockSpec((ng, 1, cg), const3),
                  pl.BlockSpec((ng, 1, cg), const3),
                  pl.BlockSpec((None, d, tn), lambda i, j: (layer, 0, j)),
                  pl.BlockSpec((tm, tn), lambda i, j: (i, j)),
                  pl.BlockSpec((1, tn), lambda i, j: (0, j))],
        out_specs=pl.BlockSpec((tm, tn), lambda i, j: (i, j)),
        out_shape=jax.ShapeDtypeStruct((s, d), F32),
        scratch_shapes=[pltpu.VMEM((tm, d), BF16)],
        compiler_params=_params(("arbitrary", "arbitrary"), 52),
        name="pool_out",
    )(p, w_grp, b_grp.reshape(ng, 1, cg), scale.reshape(ng, 1, cg), w_out, x, gate)


LANES = 128
HIGH_HALF = 0xFFFF0000


def _pack_rows(v):
    half = v.shape[1] // 2
    lo = lax.bitcast_convert_type(v[:, :half].astype(BF16).astype(F32), jnp.uint32)
    hi = lax.bitcast_convert_type(v[:, half:].astype(BF16).astype(F32), jnp.uint32)
    return (lo >> 16) | (hi & jnp.uint32(HIGH_HALF))


def _unpack_lo(u):
    return lax.bitcast_convert_type(u << 16, F32)


def _unpack_hi(u):
    return lax.bitcast_convert_type(u & jnp.uint32(HIGH_HALF), F32)


def _top_k(logits):
    n_e = logits.shape[-1]
    lane = lax.broadcasted_iota(jnp.int32, logits.shape, 1)
    work = logits
    vals, idxs = [], []
    for _ in range(TOP_K):
        m = jnp.max(work, axis=-1, keepdims=True)
        idx = jnp.min(jnp.where(work == m, lane, n_e), axis=-1, keepdims=True)
        vals.append(m)
        idxs.append(idx)
        work = jnp.where(lane == idx, -jnp.inf, work)
    return vals, idxs


def _route_kernel(x_ref, g_ref, sc_ref, sh_ref, wr_ref, br_ref,
                  hp_ref, idx_ref, w_ref, pos_ref, cnt_ref, lg_ref, carry_ref):
    tm, d = x_ref.shape
    n_e = wr_ref.shape[1]
    rows = min(NORM_ROWS, tm)

    @pl.when(pl.program_id(0) == 0)
    def _():
        carry_ref[...] = jnp.zeros(carry_ref.shape, F32)

    gs = g_ref[...] * (1.0 + sc_ref[...])
    sh = sh_ref[...]
    wr = wr_ref[...]
    wr_hi = wr.astype(BF16)
    wr_lo = (wr - wr_hi.astype(F32)).astype(BF16)
    for r in range(tm // rows):
        sl = slice(r * rows, (r + 1) * rows)
        h = _modulated_norm(x_ref[sl, :], gs, sh)
        h_hi = h.astype(BF16)
        h_lo = (h - h_hi.astype(F32)).astype(BF16)
        lg_ref[sl, :] = _dot(h_hi, wr_hi) + _dot(h_lo, wr_hi) + _dot(h_hi, wr_lo)
        hp_ref[sl, :] = _pack_rows(h)

    vals, idxs = _top_k(lg_ref[...] + br_ref[...])
    ex = [jnp.exp(v - vals[0]) for v in vals]
    den = ex[0]
    for e in ex[1:]:
        den = den + e

    lane = lax.broadcasted_iota(jnp.int32, (tm, n_e), 1)
    sel = jnp.zeros((tm, n_e), F32)
    for idx in idxs:
        sel = jnp.where(lane == idx, 1.0, sel)
    r_i = lax.broadcasted_iota(jnp.int32, (tm, tm), 0)
    c_i = lax.broadcasted_iota(jnp.int32, (tm, tm), 1)
    below = jnp.where(c_i < r_i, 1.0, 0.0).astype(BF16)
    ranks = _dot(below, sel.astype(BF16)) + carry_ref[...]

    slot = lax.broadcasted_iota(jnp.int32, (tm, TOP_K), 1)
    idx_out = jnp.zeros((tm, TOP_K), jnp.int32)
    w_out = jnp.zeros((tm, TOP_K), F32)
    pos_out = jnp.zeros((tm, TOP_K), F32)
    for k in range(TOP_K):
        pk = jnp.sum(jnp.where(lane == idxs[k], ranks, 0.0), axis=-1, keepdims=True)
        idx_out = jnp.where(slot == k, idxs[k], idx_out)
        w_out = jnp.where(slot == k, ex[k] / den, w_out)
        pos_out = jnp.where(slot == k, pk, pos_out)
    idx_ref[...] = idx_out
    w_ref[...] = w_out
    pos_ref[...] = pos_out.astype(jnp.int32)
    carry_ref[...] += jnp.sum(sel, axis=0, keepdims=True)
    cnt_ref[...] = carry_ref[...]


def _route(x, g, sc, sh, w_router, b_router):
    s, d = x.shape
    n_e = w_router.shape[1]
    half = d // 2
    tm = _tile(s, 512, 16)
    row = lambda i: (0, 0)
    tok = lambda i: (i, 0)
    return pl.pallas_call(
        _route_kernel,
        grid=(s // tm,),
        in_specs=[pl.BlockSpec((tm, d), tok),
                  pl.BlockSpec((1, d), row), pl.BlockSpec((1, d), row), pl.BlockSpec((1, d), row),
                  pl.BlockSpec((d, n_e), row), pl.BlockSpec((1, n_e), row)],
        out_specs=[pl.BlockSpec((tm, half), tok),
                   pl.BlockSpec((tm, TOP_K), tok), pl.BlockSpec((tm, TOP_K), tok),
                   pl.BlockSpec((tm, TOP_K), tok), pl.BlockSpec((1, n_e), row)],
        out_shape=[jax.ShapeDtypeStruct((s, half), jnp.uint32),
                   jax.ShapeDtypeStruct((s, TOP_K), jnp.int32),
                   jax.ShapeDtypeStruct((s, TOP_K), F32),
                   jax.ShapeDtypeStruct((s, TOP_K), jnp.int32),
                   jax.ShapeDtypeStruct((1, n_e), F32)],
        scratch_shapes=[pltpu.VMEM((tm, n_e), F32), pltpu.VMEM((1, n_e), F32)],
        compiler_params=_params(("arbitrary",), 40),
        name="moe_route",
    )(x, g, sc, sh, w_router, b_router.reshape(1, n_e))


def _dispatch_kernel(dest_ref, hp_ref, hs_hbm, row_sem):
    tm = hp_ref.shape[0]
    base = pl.program_id(0) * tm

    def row_copy(t, k):
        dst = dest_ref[(base + t) * TOP_K + k]
        return pltpu.make_async_copy(hp_ref.at[pl.ds(t, 1), :], hs_hbm.at[pl.ds(dst, 1), :], row_sem)

    def issue(t, carry):
        for k in range(TOP_K):
            row_copy(t, k).start(priority=k % 2)
        return carry

    def drain(t, carry):
        for k in range(TOP_K):
            row_copy(t, k).wait()
        return carry

    lax.fori_loop(0, tm, issue, 0, unroll=2)
    lax.fori_loop(0, tm, drain, 0, unroll=2)


def _dispatch(hp, dest):
    m = dest.shape[0]
    s, half = hp.shape
    tm = _tile(s, 256, 16)
    grid_spec = pltpu.PrefetchScalarGridSpec(
        num_scalar_prefetch=1,
        grid=(s // tm,),
        in_specs=[pl.BlockSpec((tm, half), lambda i, dest: (i, 0))],
        out_specs=pl.BlockSpec(memory_space=pl.ANY),
        scratch_shapes=[pltpu.SemaphoreType.DMA],
    )
    return pl.pallas_call(
        _dispatch_kernel, grid_spec=grid_spec,
        out_shape=jax.ShapeDtypeStruct((m, half), hp.dtype),
        compiler_params=_params(("arbitrary",), 16),
        name="moe_dispatch",
    )(dest, hp)


def _experts_kernel(vt_ref, ve_ref, lo_ref, hi_ref, hs_ref, wgu_ref, bgu_ref, wdn_ref, bdn_ref,
                    o_ref):
    v = pl.program_id(0)
    lo = lo_ref[v]
    hi = hi_ref[v]
    f, d = wdn_ref.shape
    tr, half = hs_ref.shape

    @pl.when(hi > lo)
    def _():
        hu = hs_ref[...]
        h_lo = _unpack_lo(hu).astype(BF16)
        h_hi = _unpack_hi(hu).astype(BF16)
        gu = _dot(h_lo, wgu_ref[:half, :]) + _dot(h_hi, wgu_ref[half:, :]) + bgu_ref[...]
        gt = jnp.minimum(gu[:, :f], SWIGLU_LIMIT)
        ln = jnp.clip(gu[:, f:], -SWIGLU_LIMIT, SWIGLU_LIMIT)
        act = (ln + 1.0) * gt * jax.nn.sigmoid(SWIGLU_ALPHA * gt)
        packed = _pack_rows(_dot(act.astype(BF16), wdn_ref[...]) + bdn_ref[...])
        row = lax.broadcasted_iota(jnp.int32, (tr, 1), 0)
        mine = (row >= lo) & (row < hi)

        @pl.when(lo == 0)
        def _():
            o_ref[...] = jnp.where(mine, packed, jnp.uint32(0))

        @pl.when(lo > 0)
        def _():
            o_ref[...] = jnp.where(mine, packed, o_ref[...])


EXPERT_TILE = 256


def _experts(hs, visits, w_gu, b_gu, w_dn, b_dn, tr, layer):
    vt, ve, vlo, vhi = visits
    _, n_e, d, f2 = w_gu.shape
    f = f2 // 2
    half = hs.shape[1]
    grid_spec = pltpu.PrefetchScalarGridSpec(
        num_scalar_prefetch=4,
        grid=(vt.shape[0],),
        in_specs=[pl.BlockSpec((tr, half), lambda v, vt, ve, lo, hi: (vt[v], 0)),
                  pl.BlockSpec((None, None, d, f2), lambda v, vt, ve, lo, hi: (layer, ve[v], 0, 0)),
                  pl.BlockSpec((None, 1, f2), lambda v, vt, ve, lo, hi: (ve[v], 0, 0)),
                  pl.BlockSpec((None, None, f, d), lambda v, vt, ve, lo, hi: (layer, ve[v], 0, 0)),
                  pl.BlockSpec((None, 1, d), lambda v, vt, ve, lo, hi: (ve[v], 0, 0))],
        out_specs=pl.BlockSpec((tr, half), lambda v, vt, ve, lo, hi: (vt[v], 0)),
    )
    return pl.pallas_call(
        _experts_kernel, grid_spec=grid_spec,
        out_shape=jax.ShapeDtypeStruct(hs.shape, jnp.uint32),
        compiler_params=_params(("arbitrary",), 48),
        name="moe_experts",
    )(vt, ve, vlo, vhi, hs, w_gu, b_gu.reshape(n_e, 1, f2), w_dn, b_dn.reshape(n_e, 1, d))


def _combine_kernel(dest_ref, ys_hbm, w_ref, x_ref, gate_ref, o_ref, buf_ref, row_sem):
    i = pl.program_id(0)
    last = pl.num_programs(0) - 1
    tm, d = x_ref.shape
    half = d // 2
    n_chunks = half // LANES
    slot = i % 2
    other = 1 - slot

    def row_copy(tile, sl, t, k):
        src = dest_ref[(tile * tm + t) * TOP_K + k]
        return pltpu.make_async_copy(ys_hbm.at[pl.ds(src, 1), :],
                                     buf_ref.at[sl, k, pl.ds(t, 1), :], row_sem.at[sl])

    def wait_rows(tile, sl):
        def drain(t, carry):
            for k in range(TOP_K):
                row_copy(tile, sl, t, k).wait()
            return carry
        lax.fori_loop(0, tm, drain, 0, unroll=2)

    @pl.when(i == 0)
    def _():
        def issue(t, carry):
            for k in range(TOP_K):
                row_copy(0, 0, t, k).start()
            return carry
        lax.fori_loop(0, tm, issue, 0, unroll=2)

    wait_rows(i, slot)

    nxt = jnp.minimum(i + 1, last)
    per_chunk = tm // n_chunks
    w = w_ref[...]
    wb = [jnp.broadcast_to(w[:, k:k + 1], (tm, LANES)) for k in range(TOP_K)]
    for c in range(n_chunks):
        lo = jnp.zeros((tm, LANES), F32)
        hi = jnp.zeros((tm, LANES), F32)
        for k in range(TOP_K):
            u = buf_ref[slot, k, :, c * LANES:(c + 1) * LANES]
            lo = lo + wb[k] * _unpack_lo(u)
            hi = hi + wb[k] * _unpack_hi(u)
        cl = slice(c * LANES, (c + 1) * LANES)
        ch = slice(half + c * LANES, half + (c + 1) * LANES)
        o_ref[:, cl] = x_ref[:, cl] + gate_ref[:, cl] * lo
        o_ref[:, ch] = x_ref[:, ch] + gate_ref[:, ch] * hi
        for t in range(c * per_chunk, (c + 1) * per_chunk):
            for k in range(TOP_K):
                row_copy(nxt, other, t, k).start()

    @pl.when(i == last)
    def _():
        wait_rows(last, other)


def _combine(ys, dest, w4, x, gate):
    s, d = x.shape
    tm = _tile(s, 256, 256)
    tok = lambda i, dest: (i, 0)
    grid_spec = pltpu.PrefetchScalarGridSpec(
        num_scalar_prefetch=1,
        grid=(s // tm,),
        in_specs=[pl.BlockSpec(memory_space=pl.ANY),
                  pl.BlockSpec((tm, TOP_K), tok), pl.BlockSpec((tm, d), tok),
                  pl.BlockSpec((1, d), lambda i, dest: (0, 0))],
        out_specs=pl.BlockSpec((tm, d), tok),
        scratch_shapes=[pltpu.VMEM((2, TOP_K, tm, d // 2), jnp.uint32),
                        pltpu.SemaphoreType.DMA((2,))],
    )
    return pl.pallas_call(
        _combine_kernel, grid_spec=grid_spec,
        out_shape=jax.ShapeDtypeStruct((s, d), F32),
        compiler_params=_params(("arbitrary",), 48),
        name="moe_combine",
    )(dest, ys, w4, x, gate)


def _expert_visits(off, m, tr, n_e):
    nt = m // tr
    pts = jnp.sort(jnp.concatenate([jnp.arange(nt, dtype=jnp.int32) * tr, off[:n_e]]))
    nxt = jnp.concatenate([pts[1:], jnp.full((1,), m, jnp.int32)])
    vt = jnp.minimum(pts // tr, nt - 1)
    ve = jnp.minimum(jnp.sum(off[None, 1:] <= pts[:, None], axis=1), n_e - 1).astype(jnp.int32)
    return vt, ve, pts - vt * tr, nxt - vt * tr


def _moe(x, g, sc, sh, gate, w_router, b_router, w_gu, b_gu, w_dn, b_dn, layer):
    s, d = x.shape
    n_e = w_router.shape[1]
    m = TOP_K * s
    tr = _tile(m, EXPERT_TILE, 16)
    hp, idx4, w4, pos4, counts = _route(x, g, sc, sh, w_router, b_router)
    counts = counts.reshape(n_e).astype(jnp.int32)
    off = jnp.concatenate([jnp.zeros((1,), jnp.int32), jnp.cumsum(counts)])
    dest = (off[idx4] + pos4).reshape(m)
    hs = _dispatch(hp, dest)
    ys = _experts(hs, _expert_visits(off, m, tr, n_e), w_gu, b_gu, w_dn, b_dn, tr, layer)
    return _combine(ys, dest, w4, x, gate)


def _prep_gu_kernel(w_ref, o_ref):
    n = w_ref.shape[1]
    src = lax.broadcasted_iota(jnp.int32, (n, n), 0)
    dst = lax.broadcasted_iota(jnp.int32, (n, n), 1)
    want = jnp.where(dst < n // 2, 2 * dst, 2 * (dst - n // 2) + 1)
    perm = jnp.where(src == want, 1.0, 0.0).astype(BF16)
    o_ref[...] = _dot(w_ref[...].astype(BF16), perm).astype(BF16)


def _prep_gu(w_gu):
    shape = w_gu.shape
    n = shape[-1]
    rows = w_gu.size // n
    tm = _tile(rows, 2048, 16)
    out = pl.pallas_call(
        _prep_gu_kernel, grid=(rows // tm,),
        in_specs=[pl.BlockSpec((tm, n), lambda i: (i, 0))],
        out_specs=pl.BlockSpec((tm, n), lambda i: (i, 0)),
        out_shape=jax.ShapeDtypeStruct((rows, n), BF16),
        compiler_params=_params(("arbitrary",), 32),
        name="prep_expert_up",
    )(w_gu.reshape(rows, n))
    return out.reshape(shape)


def _fcum_kernel(fl_ref, o_ref, carry_ref):
    @pl.when(pl.program_id(0) == 0)
    def _():
        carry_ref[...] = jnp.zeros(carry_ref.shape, F32)

    x = fl_ref[...]
    log_f = jnp.minimum(x, 0.0) - jnp.log1p(jnp.exp(-jnp.abs(x)))
    tm = x.shape[0]
    r = lax.broadcasted_iota(jnp.int32, (tm, tm), 0)
    c = lax.broadcasted_iota(jnp.int32, (tm, tm), 1)
    tri = jnp.where(c <= r, 1.0, 0.0).astype(BF16)
    hi, mid, lo = _split3(log_f)
    out = (_dot(tri, hi) + _dot(tri, mid) + _dot(tri, lo)) + carry_ref[...]
    o_ref[...] = out
    carry_ref[...] = out[tm - 1:tm, :]


def _forget_cumsum(fl):
    s, h = fl.shape
    tm = _tile(s, 256, 8)
    return pl.pallas_call(
        _fcum_kernel, grid=(s // tm,),
        in_specs=[pl.BlockSpec((tm, h), lambda i: (i, 0))],
        out_specs=pl.BlockSpec((tm, h), lambda i: (i, 0)),
        out_shape=jax.ShapeDtypeStruct((s, h), F32),
        scratch_shapes=[pltpu.VMEM((1, h), F32)],
        compiler_params=_params(("arbitrary",), 16),
        name="forget_cumsum",
    )(fl)


AUG_HEAD_CHUNK = 4
N_BIAS_PIECES = 3


def _aug_kernel(src_ref, f4_ref, g_ref, shift_ref, sel_ref, o_ref, *, mult, n_heads):
    tm = src_ref.shape[0]
    gm = g_ref[...] * mult
    f = f4_ref[...] * LOG2E - shift_ref[...]
    hi, mid, lo = [p.astype(F32) for p in _split3(f)]
    lane = lax.broadcasted_iota(jnp.int32, f.shape, 1)
    pieces = jnp.where(lane < n_heads, hi,
                       jnp.where(lane < 2 * n_heads, mid,
                                 jnp.where(lane < 3 * n_heads, lo, 1.0))).astype(BF16)
    chunk = min(AUG_HEAD_CHUNK, n_heads)
    for h0 in range(0, n_heads, chunk):
        bias = _dot(pieces, sel_ref[:, h0 * HEAD_DIM:(h0 + chunk) * HEAD_DIM])
        for h in range(h0, h0 + chunk):
            t = src_ref[:, h * HEAD_DIM:(h + 1) * HEAD_DIM].astype(F32)
            ms = jnp.mean(t * t, axis=-1, keepdims=True)
            o_ref[:, h * AUG_DIM:h * AUG_DIM + HEAD_DIM] = (t * lax.rsqrt(ms + EPS) * gm).astype(BF16)
            o_ref[:, h * AUG_DIM + HEAD_DIM:(h + 1) * AUG_DIM] = (
                bias[:, (h - h0) * HEAD_DIM:(h - h0 + 1) * HEAD_DIM].astype(BF16))


def _bias_selector(n_heads, is_query):
    row = jnp.arange(HEAD_DIM)[:, None]
    col = jnp.arange(n_heads * HEAD_DIM)[None, :]
    h, c = col // HEAD_DIM, col % HEAD_DIM
    ones_row = N_BIAS_PIECES * n_heads + h
    if is_query:
        piece = (c < N_BIAS_PIECES) & (row == c * n_heads + h)
        ones = (c >= N_BIAS_PIECES) & (c < 2 * N_BIAS_PIECES) & (row == ones_row)
        sel = jnp.where(piece | ones, 1.0, 0.0)
    else:
        ones = (c < N_BIAS_PIECES) & (row == ones_row)
        piece = ((c >= N_BIAS_PIECES) & (c < 2 * N_BIAS_PIECES)
                 & (row == (c - N_BIAS_PIECES) * n_heads + h))
        sel = jnp.where(ones, 1.0, 0.0) - jnp.where(piece, 1.0, 0.0)
    return sel.astype(BF16)


def _augment(src, f4, g, shift, *, n_heads, mult, is_query):
    assert (N_BIAS_PIECES + 1) * n_heads <= HEAD_DIM
    s = src.shape[0]
    d = n_heads * HEAD_DIM
    tm = _tile(s, 256, 16)
    const = lambda i: (0, 0)
    return pl.pallas_call(
        functools.partial(_aug_kernel, mult=mult, n_heads=n_heads),
        grid=(s // tm,),
        in_specs=[pl.BlockSpec((tm, d), lambda i: (i, 0)),
                  pl.BlockSpec((tm, HEAD_DIM), lambda i: (i, 0)),
                  pl.BlockSpec((1, HEAD_DIM), const),
                  pl.BlockSpec((1, 1), const),
                  pl.BlockSpec((HEAD_DIM, d), const)],
        out_specs=pl.BlockSpec((tm, n_heads * AUG_DIM), lambda i: (i, 0)),
        out_shape=jax.ShapeDtypeStruct((s, n_heads * AUG_DIM), BF16),
        compiler_params=_params(("arbitrary",), 32),
        name="augment_q" if is_query else "augment_k",
    )(src, f4, g.reshape(1, HEAD_DIM), jnp.asarray(shift, F32).reshape(1, 1),
      _bias_selector(n_heads, is_query))


def _aug_v_kernel(v_ref, o_ref):
    tm = v_ref.shape[0]
    one = jnp.ones((tm, HEAD_DIM), BF16)
    for h in range(v_ref.shape[1] // HEAD_DIM):
        o_ref[:, h * AUG_DIM:h * AUG_DIM + HEAD_DIM] = v_ref[:, h * HEAD_DIM:(h + 1) * HEAD_DIM]
        o_ref[:, h * AUG_DIM + HEAD_DIM:(h + 1) * AUG_DIM] = one


def _augment_v(kv, d):
    s = kv.shape[0]
    n_heads = d // HEAD_DIM
    tm = _tile(s, 256, 16)
    return pl.pallas_call(
        _aug_v_kernel, grid=(s // tm,),
        in_specs=[pl.BlockSpec((tm, d), lambda i: (i, 1))],
        out_specs=pl.BlockSpec((tm, n_heads * AUG_DIM), lambda i: (i, 0)),
        out_shape=jax.ShapeDtypeStruct((s, n_heads * AUG_DIM), BF16),
        compiler_params=_params(("arbitrary",), 32),
        name="augment_v",
    )(kv)


def _attn_kernel(start_ref, q_ref, k_ref, v_ref, o_ref, m_ref, l_ref, acc_ref, *, tk):
    h = pl.program_id(0)
    qi = pl.program_id(1)
    n_sub = q_ref.shape[0] // tk
    m_ref[...] = jnp.full(m_ref.shape, MASK_VALUE, F32)
    l_ref[...] = jnp.zeros(l_ref.shape, F32)
    acc_ref[...] = jnp.zeros(acc_ref.shape, F32)

    def sub_step(a, kb, on_diagonal):
        qrows = slice(a * tk, (a + 1) * tk)
        krows = pl.ds(pl.multiple_of(kb * tk, tk), tk)
        s = lax.dot_general(q_ref[qrows, :], k_ref[krows, :], (((1,), (1,)), ((), ())),
                            preferred_element_type=F32)
        if on_diagonal:
            r = lax.broadcasted_iota(jnp.int32, s.shape, 0)
            c = lax.broadcasted_iota(jnp.int32, s.shape, 1)
            s = jnp.where(c <= r, s, MASK_VALUE)
        m_prev = m_ref[qrows, :]
        m_new = jnp.maximum(m_prev, jnp.max(s, axis=-1, keepdims=True))
        alpha = jnp.exp2(m_prev - m_new)
        p = jnp.exp2(s - m_new)
        l_ref[qrows, :] = alpha * l_ref[qrows, :] + jnp.sum(p, axis=-1, keepdims=True)
        acc_ref[qrows, :] = alpha * acc_ref[qrows, :] + _dot(p.astype(BF16), v_ref[krows, :])
        m_ref[qrows, :] = m_new

    def body(kb, carry):
        for a in range(n_sub):
            sub_step(a, kb, False)
        return carry

    first_diag = qi * n_sub
    lax.fori_loop(start_ref[h, qi], first_diag, body, 0)
    for a in range(n_sub):
        for b in range(a):
            sub_step(a, first_diag + b, False)
        sub_step(a, first_diag + a, True)
    o_ref[...] = (acc_ref[...] / l_ref[...]).astype(o_ref.dtype)


def _attention(q_aug, k_aug, kv, start, n_heads, tq, tk):
    s = q_aug.shape[0]
    nq = s // tq
    grid_spec = pltpu.PrefetchScalarGridSpec(
        num_scalar_prefetch=1,
        grid=(n_heads, nq),
        in_specs=[pl.BlockSpec((tq, AUG_DIM), lambda h, qi, st: (qi, h)),
                  pl.BlockSpec((s, AUG_DIM), lambda h, qi, st: (0, h)),
                  pl.BlockSpec((s, HEAD_DIM), lambda h, qi, st: (0, n_heads + h))],
        out_specs=pl.BlockSpec((tq, HEAD_DIM), lambda h, qi, st: (qi, h)),
        scratch_shapes=[pltpu.VMEM((tq, 1), F32), pltpu.VMEM((tq, 1), F32),
                        pltpu.VMEM((tq, HEAD_DIM), F32)],
    )
    return pl.pallas_call(
        functools.partial(_attn_kernel, tk=tk), grid_spec=grid_spec,
        out_shape=jax.ShapeDtypeStruct((s, n_heads * HEAD_DIM), BF16),
        compiler_params=_params(("arbitrary", "arbitrary"), 48),
        name="fox_attention",
    )(start, q_aug, k_aug, kv)


def _attn_bounded_kernel(start_ref, q_ref, k_ref, v_ref, o_ref, acc_ref, *, tk):
    h = pl.program_id(0)
    qi = pl.program_id(1)
    n_sub = q_ref.shape[0] // tk
    acc_ref[...] = jnp.zeros(acc_ref.shape, F32)

    def sub_step(a, kb, on_diagonal):
        qrows = slice(a * tk, (a + 1) * tk)
        krows = pl.ds(pl.multiple_of(kb * tk, tk), tk)
        s = lax.dot_general(q_ref[qrows, :], k_ref[krows, :], (((1,), (1,)), ((), ())),
                            preferred_element_type=F32)
        p = jnp.exp2(s)
        if on_diagonal:
            r = lax.broadcasted_iota(jnp.int32, s.shape, 0)
            c = lax.broadcasted_iota(jnp.int32, s.shape, 1)
            p = jnp.where(c <= r, p, 0.0)
        acc_ref[qrows, :] += _dot(p.astype(BF16), v_ref[krows, :])

    first_diag = qi * n_sub
    start = start_ref[h, qi]
    n_pairs = (first_diag - start) // 2

    def pair_body(j, carry):
        for kb in (start + 2 * j, start + 2 * j + 1):
            for a in range(n_sub):
                sub_step(a, kb, False)
        return carry

    lax.fori_loop(0, n_pairs, pair_body, 0)

    @pl.when(start + 2 * n_pairs < first_diag)
    def _():
        for a in range(n_sub):
            sub_step(a, first_diag - 1, False)

    for a in range(n_sub):
        for b in range(a):
            sub_step(a, first_diag + b, False)
        sub_step(a, first_diag + a, True)
    acc = acc_ref[...]
    o_ref[...] = (acc[:, :HEAD_DIM] / acc[:, HEAD_DIM:]).astype(o_ref.dtype)


def _attention_bounded(q_aug, k_aug, v_aug, start, n_heads, tq, tk):
    s = q_aug.shape[0]
    grid_spec = pltpu.PrefetchScalarGridSpec(
        num_scalar_prefetch=1,
        grid=(n_heads, s // tq),
        in_specs=[pl.BlockSpec((tq, AUG_DIM), lambda h, qi, st: (qi, h)),
                  pl.BlockSpec((s, AUG_DIM), lambda h, qi, st: (0, h)),
                  pl.BlockSpec((s, AUG_DIM), lambda h, qi, st: (0, h))],
        out_specs=pl.BlockSpec((tq, HEAD_DIM), lambda h, qi, st: (qi, h)),
        scratch_shapes=[pltpu.VMEM((tq, AUG_DIM), F32)],
    )
    return pl.pallas_call(
        functools.partial(_attn_bounded_kernel, tk=tk), grid_spec=grid_spec,
        out_shape=jax.ShapeDtypeStruct((s, n_heads * HEAD_DIM), BF16),
        compiler_params=_params(("arbitrary", "arbitrary"), 52),
        name="fox_attention_bounded",
    )(start, q_aug, k_aug, v_aug)


MAX_BOUNDED_LOGIT_RANGE = 80.0


def _logit_bound(q_norm_g, k_norm_g):
    return HEAD_DIM ** 0.5 * jnp.max(jnp.abs(q_norm_g * k_norm_g))


def _attention_start_blocks(f_cum, q_norm_g, k_norm_g, tq, tk):
    s, n_heads = f_cum.shape
    nq = s // tq
    nk = s // tk
    thresh = F32_EXP_UNDERFLOW + 2.0 * _logit_bound(q_norm_g, k_norm_g) + 1.0
    f_max_q = jnp.max(f_cum.reshape(nq, tq, n_heads), axis=1).T
    f_min_k = jnp.min(f_cum.reshape(nk, tk, n_heads), axis=1).T
    needed = (f_max_q[:, :, None] - f_min_k[:, None, :]) >= -thresh
    first_diag = jnp.arange(nq) * (tq // tk)
    needed = needed | (jnp.arange(nk)[None, None, :] >= first_diag[None, :, None])
    return jnp.argmax(needed, axis=-1).astype(jnp.int32)


def _gateout_kernel(o_ref, gp_ref, wo_ref, x_ref, gate_ref, out_ref, a_ref):
    @pl.when(pl.program_id(1) == 0)
    def _():
        a = o_ref[...].astype(F32) * jax.nn.sigmoid(gp_ref[...].astype(F32))
        a_ref[...] = a.astype(a_ref.dtype)

    out_ref[...] = x_ref[...] + gate_ref[...] * _dot(a_ref[...], wo_ref[...])


def _gate_out(o, qg, w_o, x, gate, layer):
    s, d = x.shape
    tm = _tile(s, 512, 16)
    tn = _tile(d, WIDE_TN, 128)
    return pl.pallas_call(
        _gateout_kernel,
        grid=(s // tm, d // tn),
        in_specs=[pl.BlockSpec((tm, d), lambda i, j: (i, 0)),
                  pl.BlockSpec((tm, d), lambda i, j: (i, 1)),
                  pl.BlockSpec((None, d, tn), lambda i, j: (layer, 0, j)),
                  pl.BlockSpec((tm, tn), lambda i, j: (i, j)),
                  pl.BlockSpec((1, tn), lambda i, j: (0, j))],
        out_specs=pl.BlockSpec((tm, tn), lambda i, j: (i, j)),
        out_shape=jax.ShapeDtypeStruct((s, d), F32),
        scratch_shapes=[pltpu.VMEM((tm, d), BF16)],
        compiler_params=_params(("arbitrary", "arbitrary"), 48),
        name="gate_out",
    )(o, qg, w_o, x, gate)


ATTN_KV_BLOCK = 512
ATTN_Q_SUBTILES = 2


def kernel(x, c, w_ada, b_ada, ada_table, norm1_g, norm2_g, a_w_in, a_w_grp, a_b_grp, a_scale,
           a_w_out, kv_norm_g, w_kvf, b_f, k_norm_g, b_w_qg, q_norm_g, b_w_o, moe_w_router,
           moe_b_router, moe_w_gu, moe_b_gu, moe_w_dn, moe_b_dn):
    b, s, d = x.shape
    assert b == 1, "single-sequence trunk"
    depth = ada_table.shape[0]
    n_mod = ada_table.shape[1]
    n_a = a_w_in.shape[0]
    n_heads = d // HEAD_DIM
    tk = _tile(s, ATTN_KV_BLOCK, 16)
    tq = tk * ATTN_Q_SUBTILES if s % (tk * ATTN_Q_SUBTILES) == 0 else tk
    xs = x.reshape(s, d)

    mod = _ada(c, w_ada, b_ada).reshape(n_mod, d)
    zeros_row = jnp.zeros((1, d), F32)
    w_gu_all = _prep_gu(moe_w_gu)
    w_dn_all = moe_w_dn.astype(BF16)
    w_in_all = a_w_in.astype(BF16)
    w_grp_all = a_w_grp.astype(BF16)
    w_out_all = a_w_out.astype(BF16)
    w_qg_all = b_w_qg.astype(BF16)
    w_o_all = b_w_o.astype(BF16)
    w_kvf_b = w_kvf.astype(BF16).reshape(1, d, w_kvf.shape[1])
    b_gu_all = jnp.concatenate([moe_b_gu[..., 0::2], moe_b_gu[..., 1::2]], axis=-1)
    k_aug = v_aug = kv = f_cum = f4 = None

    for layer in range(depth):
        m = mod + ada_table[layer]
        shift1, scale1, gate1, shift2, scale2, gate2 = [m[i:i + 1] for i in range(n_mod)]
        g1 = norm1_g[layer].reshape(1, d)
        g2 = norm2_g[layer].reshape(1, d)
        if layer < n_a:
            p = _in_proj_pool(xs, g1, scale1, shift1, w_in_all, layer)
            xs = _pool_out(p, w_grp_all, a_b_grp[layer], a_scale[layer], w_out_all, xs, gate1, layer)
        else:
            j = layer - n_a
            qg = _norm_matmul(xs, g1, scale1, shift1, w_qg_all, j, 2 * d)
            bound2 = LOG2E * (_logit_bound(q_norm_g[j], k_norm_g) + 1.0)
            bounded = 2.0 * bound2 <= MAX_BOUNDED_LOGIT_RANGE
            q_aug = _augment(qg, f4, q_norm_g[j], jnp.where(bounded, bound2, 0.0),
                             n_heads=n_heads, mult=HEAD_DIM ** -0.5 * LOG2E, is_query=True)
            start = _attention_start_blocks(f_cum, q_norm_g[j], k_norm_g, tq, tk)
            o = lax.cond(
                bounded,
                lambda qa, ka, va, kv_, st: _attention_bounded(qa, ka, va, st, n_heads, tq, tk),
                lambda qa, ka, va, kv_, st: _attention(qa, ka, kv_, st, n_heads, tq, tk),
                q_aug, k_aug, v_aug, kv, start)
            xs = _gate_out(o, qg, w_o_all, xs, gate1, j)

        xs = _moe(xs, g2, scale2, shift2, gate2, moe_w_router[layer], moe_b_router[layer],
                  w_gu_all, b_gu_all[layer], w_dn_all, moe_b_dn[layer], layer)

        if layer == n_a - 1:
            kv, fl = _norm_matmul(xs, kv_norm_g.reshape(1, d), zeros_row, zeros_row,
                                  w_kvf_b, 0, 2 * d,
                                  side=(w_kvf_b[0, :, 2 * d:], b_f.reshape(1, n_heads)))
            f_cum = _forget_cumsum(fl)
            f4 = jnp.pad(jnp.tile(f_cum, (1, N_BIAS_PIECES + 1)),
                         ((0, 0), (0, HEAD_DIM - (N_BIAS_PIECES + 1) * n_heads)))
            k_aug = _augment(kv, f4, k_norm_g, 0.0, n_heads=n_heads, mult=1.0, is_query=False)
            v_aug = _augment_v(kv, d)

    return xs.reshape(b, s, d)
```
